```python
import math
import jax, jax.numpy as jnp
from jax import lax
import numpy as np

D_MODEL = 1024
BATCH = 32
SEQ = 256
DEPTH = 1
DEC_BATCH = 8
DEC_SEQ = 2048
PAST_LEN = 512

GRID_W = 64
MIX_WIDTH = D_MODEL
GM_HEADS = 4
GM_HEAD_DIM = 128
GM_WIDTH = GM_HEADS * GM_HEAD_DIM
GM_CHUNK = 128
DN_HEADS = 4
DN_HEAD_K = 128
DN_HEAD_V = 128
DN_WIDTH = DN_HEADS * DN_HEAD_V
DN_QKV = 2 * DN_HEADS * DN_HEAD_K + DN_HEADS * DN_HEAD_V
DN_DIRS = 2
DN_CHUNK = 64
DN_CONV = 3
ROPE_BASE = 10000.0
IN_COLS = 2 * GM_WIDTH + DN_QKV + DN_WIDTH + 2 * DN_DIRS * DN_HEADS
IN_SPLITS = (GM_WIDTH, 2 * GM_WIDTH, 2 * GM_WIDTH + DN_QKV, 2 * GM_WIDTH + DN_QKV + DN_WIDTH,
             2 * GM_WIDTH + DN_QKV + DN_WIDTH + DN_DIRS * DN_HEADS)
PEER_HEADS = 8
PEER_NKEYS = 128
PEER_N = PEER_NKEYS * PEER_NKEYS
PEER_DKEY = 256
PEER_TOPK = 16
PEER_BLOCK = 128
EPS = 1e-6

kernel_name = "hymba_gmlp_gdn_peer_diffusion_step"


def rms_norm(x, gain):
    xf = x.astype(jnp.float32)
    y = xf * lax.rsqrt(jnp.mean(xf * xf, axis=-1, keepdims=True) + EPS)
    return (y * gain.astype(jnp.float32)).astype(x.dtype)


def l2_norm(x):
    return x * lax.rsqrt(jnp.sum(x * x, axis=-1, keepdims=True) + EPS)


def chunk_spatial_gate(u, v, ws, bs):
    B, L, H, dh = v.shape
    n = L // GM_CHUNK
    vf = v.astype(jnp.float32)
    vf = vf * lax.rsqrt(jnp.mean(vf * vf, axis=-1, keepdims=True) + EPS)
    vc = vf.reshape(B, n, GM_CHUNK, H, dh)
    s = jnp.einsum('hij,bnjhd->bnihd', ws.astype(jnp.float32), vc) + bs.astype(jnp.float32).T[None, None, :, :, None]
    return (u.astype(jnp.float32) * s.reshape(B, L, H, dh)).astype(u.dtype)


def short_conv(x, w):
    C = x.shape[-1]
    pad = DN_CONV // 2
    return lax.conv_general_dilated(x, w[:, None, :].astype(x.dtype), window_strides=(1,),
                                    padding=((pad, pad),), dimension_numbers=('NWC', 'WIO', 'NWC'),
                                    feature_group_count=C)


def grid_rope(x):
    L = x.shape[1]
    rows = L // GRID_W
    r, cidx = jnp.meshgrid(jnp.arange(rows), jnp.arange(GRID_W), indexing='ij')
    half = DN_HEAD_K // 2
    quarter = half // 2
    inv = ROPE_BASE ** (-jnp.arange(quarter, dtype=jnp.float32) / quarter)
    ang_r = r.reshape(-1).astype(jnp.float32)[:, None] * inv
    ang_c = cidx.reshape(-1).astype(jnp.float32)[:, None] * inv

    def rot(xh, ang):
        x1, x2 = xh[..., :quarter], xh[..., quarter:]
        cos = jnp.cos(ang)[None, :, None, :]
        sin = jnp.sin(ang)[None, :, None, :]
        return jnp.concatenate([x1 * cos - x2 * sin, x1 * sin + x2 * cos], axis=-1)

    return jnp.concatenate([rot(x[..., :half], ang_r), rot(x[..., half:], ang_c)], axis=-1)


def gated_delta_chunked(q, k, v, g, beta, s0):
    B, L, H, K = q.shape
    V = v.shape[-1]
    C = DN_CHUNK
    n = L // C

    def blk(t):
        t = t.reshape(B, n, C, H, *t.shape[3:])
        return jnp.moveaxis(t, 3, 1)

    qc, kc, vc, gc, bc = blk(q), blk(k), blk(v), blk(g), blk(beta)
    gcum = jnp.cumsum(gc, axis=-1)
    idx = jnp.arange(C)
    causal = idx[:, None] >= idx[None, :]
    strict = idx[:, None] > idx[None, :]
    decay = jnp.exp(jnp.where(causal, gcum[..., :, None] - gcum[..., None, :], -jnp.inf))
    kb = kc * bc[..., None]
    a = jnp.where(strict, jnp.einsum('bhnik,bhnjk->bhnij', kb, kc) * decay, 0.0)
    eye = jnp.eye(C, dtype=q.dtype)
    lhs = a + eye
    t_inv = lax.linalg.triangular_solve(lhs, jnp.broadcast_to(eye, lhs.shape), left_side=True, lower=True)
    w = jnp.einsum('bhnij,bhnjk->bhnik', t_inv, kb * jnp.exp(gcum)[..., None])
    u = jnp.einsum('bhnij,bhnjv->bhniv', t_inv, vc * bc[..., None])
    attn = jnp.einsum('bhnik,bhnjk->bhnij', qc, kc) * decay
    q_dec = qc * jnp.exp(gcum)[..., None]
    k_dec = kc * jnp.exp(gcum[..., -1:] - gcum)[..., None]
    g_last = jnp.exp(gcum[..., -1])

    def step(s, xs):
        w_i, u_i, attn_i, qd_i, kd_i, gl_i = xs
        v_new = u_i - jnp.einsum('bhck,bhkv->bhcv', w_i, s)
        o = jnp.einsum('bhck,bhkv->bhcv', qd_i, s) + jnp.einsum('bhij,bhjv->bhiv', attn_i, v_new)
        s = s * gl_i[..., None, None] + jnp.einsum('bhck,bhcv->bhkv', kd_i, v_new)
        return s, o

    xs = tuple(jnp.moveaxis(t, 2, 0) for t in (w, u, attn, q_dec, k_dec, g_last))
    s_final, o = lax.scan(step, s0, xs)
    o = jnp.transpose(o, (1, 0, 3, 2, 4)).reshape(B, L, H, V)
    return o, s_final


def deltanet_mixer(p_qkv, p_z, p_a, p_b, conv_w, a_log, dt_bias, norm_w, s0, use_rope):
    B, L, _ = p_qkv.shape
    qkv = jax.nn.silu(short_conv(p_qkv, conv_w)).astype(jnp.float32)
    q, k, v = jnp.split(qkv, [DN_HEADS * DN_HEAD_K, 2 * DN_HEADS * DN_HEAD_K], axis=-1)
    q = l2_norm(q.reshape(B, L, DN_HEADS, DN_HEAD_K))
    k = l2_norm(k.reshape(B, L, DN_HEADS, DN_HEAD_K))
    if use_rope:
        q = grid_rope(q)
        k = grid_rope(k)
    q = q * (DN_HEAD_K ** -0.5)
    v = v.reshape(B, L, DN_HEADS, DN_HEAD_V)
    a = p_a.astype(jnp.float32).reshape(B, L, DN_DIRS, DN_HEADS)
    b = p_b.astype(jnp.float32).reshape(B, L, DN_DIRS, DN_HEADS)
    g = -jnp.exp(a_log.astype(jnp.float32)) * jax.nn.softplus(a + dt_bias.astype(jnp.float32))
    beta = jax.nn.sigmoid(b)
    s0 = s0.astype(jnp.float32)
    o_f, s_f = gated_delta_chunked(q, k, v, g[:, :, 0], beta[:, :, 0], s0[:, 0])
    flip = lambda t: t[:, ::-1]
    o_b, s_b = gated_delta_chunked(flip(q), flip(k), flip(v), flip(g[:, :, 1]), flip(beta[:, :, 1]), s0[:, 1])
    o = o_f + flip(o_b)
    z = p_z.astype(jnp.float32).reshape(B, L, DN_HEADS, DN_HEAD_V)
    o = rms_norm(o, norm_w) * jax.nn.silu(z)
    return o.reshape(B, L, DN_WIDTH).astype(p_qkv.dtype), jnp.stack([s_f, s_b], axis=1)


def peer_ffn(h, wq, keys, u_tab, v_tab):
    B, L, D = h.shape
    hb = h.reshape(-1, PEER_BLOCK, D)

    def block(xb):
        T = xb.shape[0]
        q = (xb @ wq).reshape(T, PEER_HEADS, 2, PEER_DKEY // 2)
        s = jnp.einsum('thpk,hpnk->thpn', q, keys).astype(jnp.float32)
        sv, si = lax.top_k(s, PEER_TOPK)
        cand = (sv[:, :, 0, :, None] + sv[:, :, 1, None, :]).reshape(T, PEER_HEADS, PEER_TOPK * PEER_TOPK)
        cv, ci = lax.top_k(cand, PEER_TOPK)
        i1 = jnp.take_along_axis(si[:, :, 0], ci // PEER_TOPK, axis=-1)
        i2 = jnp.take_along_axis(si[:, :, 1], ci % PEER_TOPK, axis=-1)
        e = (i1 * PEER_NKEYS + i2).reshape(T, PEER_HEADS * PEER_TOPK)
        gate = jax.nn.softmax(cv, axis=-1).reshape(T, PEER_HEADS * PEER_TOPK)
        act = jax.nn.gelu(jnp.einsum('ted,td->te', u_tab[e], xb).astype(jnp.float32))
        coef = (gate * act).astype(xb.dtype)
        return jnp.einsum('te,ted->td', coef, v_tab[e])

    return lax.map(block, hb).reshape(B, L, D)


def trunk_layer(x, mod, s0, use_rope, w_in, w_out, g_pre_mix, g_post_mix, g_pre_ffn, g_post_ffn,
                gm_ws, gm_bs, dn_conv, dn_a_log, dn_dt_bias, dn_norm, peer_wq, peer_keys, peer_u, peer_v):
    B, L, _ = x.shape
    shift1, scale1, gate1, shift2, scale2, gate2 = jnp.split(mod, 6, axis=-1)
    h = rms_norm(x, g_pre_mix) * (1 + scale1) + shift1
    p = h @ w_in
    p_gu, p_gv, p_qkv, p_z, p_a, p_b = jnp.split(p, IN_SPLITS, axis=-1)
    gu = jax.nn.gelu(p_gu).reshape(B, L, GM_HEADS, GM_HEAD_DIM)
    gv = jax.nn.gelu(p_gv).reshape(B, L, GM_HEADS, GM_HEAD_DIM)
    o_a = chunk_spatial_gate(gu, gv, gm_ws, gm_bs).reshape(B, L, GM_WIDTH)
    o_b, s_new = deltanet_mixer(p_qkv, p_z, p_a, p_b, dn_conv, dn_a_log, dn_dt_bias, dn_norm, s0, use_rope)
    y = jnp.concatenate([o_a, o_b], axis=-1) @ w_out
    x = x + gate1 * rms_norm(y, g_post_mix)
    h2 = rms_norm(x, g_pre_ffn) * (1 + scale2) + shift2
    x = x + gate2 * rms_norm(peer_ffn(h2, peer_wq, peer_keys, peer_u, peer_v), g_post_ffn)
    return x, s_new


def setup_inputs(seed: int = 0) -> dict:
    key = jax.random.key(seed)
    ks = jax.random.split(key, 24)
    f32 = jnp.float32
    nrm = lambda k, shape, s: jax.random.normal(k, shape, f32) * s
    x_prompt = nrm(ks[0], (BATCH, SEQ, D_MODEL), 1.0)
    x_sample = nrm(ks[1], (DEC_BATCH, DEC_SEQ, D_MODEL), 1.0)
    state_delta = nrm(ks[2], (DEC_BATCH, DEPTH, DN_DIRS, DN_HEADS, DN_HEAD_K, DN_HEAD_V), DN_HEAD_K ** -0.5)
    c = nrm(ks[3], (DEC_BATCH, D_MODEL), 1.0)
    c_ctx = nrm(ks[4], (D_MODEL,), 1.0)
    w_mod = nrm(ks[5], (DEPTH, D_MODEL, 6 * D_MODEL), 0.5 * D_MODEL ** -0.5)
    b_mod = nrm(ks[6], (DEPTH, 6 * D_MODEL), 0.02)
    g_pre_mix = 1.0 + nrm(ks[7], (DEPTH, D_MODEL), 0.02)
    g_post_mix = 1.0 + nrm(ks[8], (DEPTH, D_MODEL), 0.02)
    g_pre_ffn = 1.0 + nrm(ks[9], (DEPTH, D_MODEL), 0.02)
    g_post_ffn = 1.0 + nrm(ks[10], (DEPTH, D_MODEL), 0.02)
    w_in = nrm(ks[11], (DEPTH, D_MODEL, IN_COLS), D_MODEL ** -0.5)
    w_out = nrm(ks[12], (DEPTH, MIX_WIDTH, D_MODEL), MIX_WIDTH ** -0.5)
    gm_ws = nrm(ks[13], (DEPTH, GM_HEADS, GM_CHUNK, GM_CHUNK), GM_CHUNK ** -0.5)
    gm_bs = 1.0 + nrm(ks[14], (DEPTH, GM_HEADS, GM_CHUNK), 0.02)
    dn_conv = nrm(ks[15], (DEPTH, DN_CONV, DN_QKV), DN_CONV ** -0.5)
    dn_a_log = jnp.log(jax.random.uniform(ks[16], (DEPTH, DN_DIRS, DN_HEADS), f32, 1.0, 16.0))
    dt = jnp.exp(jax.random.uniform(ks[17], (DEPTH, DN_DIRS, DN_HEADS), f32, math.log(1e-3), math.log(1e-1)))
    dn_dt_bias = dt + jnp.log(-jnp.expm1(-dt))
    dn_norm = 1.0 + nrm(ks[18], (DEPTH, DN_HEAD_V), 0.02)
    peer_wq = nrm(ks[19], (DEPTH, D_MODEL, PEER_HEADS * PEER_DKEY), D_MODEL ** -0.5)
    peer_keys = nrm(ks[20], (DEPTH, PEER_HEADS, 2, PEER_NKEYS, PEER_DKEY // 2), (PEER_DKEY // 2) ** -0.5)
    peer_u = nrm(ks[21], (DEPTH, PEER_N, D_MODEL), D_MODEL ** -0.5)
    peer_v = nrm(ks[22], (DEPTH, PEER_N, D_MODEL), D_MODEL ** -0.5)
    return {"x_prompt": x_prompt, "x_sample": x_sample, "state_delta": state_delta, "c": c, "c_ctx": c_ctx,
            "w_mod": w_mod, "b_mod": b_mod, "g_pre_mix": g_pre_mix, "g_post_mix": g_post_mix,
            "g_pre_ffn": g_pre_ffn, "g_post_ffn": g_post_ffn, "w_in": w_in, "w_out": w_out,
            "gm_ws": gm_ws, "gm_bs": gm_bs, "dn_conv": dn_conv, "dn_a_log": dn_a_log, "dn_dt_bias": dn_dt_bias,
            "dn_norm": dn_norm, "peer_wq": peer_wq, "peer_keys": peer_keys, "peer_u": peer_u, "peer_v": peer_v}


def reference(x_prompt, x_sample, state_delta, c, c_ctx, w_mod, b_mod, g_pre_mix, g_post_mix, g_pre_ffn,
              g_post_ffn, w_in, w_out, gm_ws, gm_bs, dn_conv, dn_a_log, dn_dt_bias, dn_norm, peer_wq, peer_keys,
              peer_u, peer_v):
    xp = x_prompt
    xs = x_sample
    new_states = []
    for l in range(DEPTH):
        mod_ctx = (jax.nn.silu(c_ctx) @ w_mod[l] + b_mod[l])[None, None, :]
        mod_lat = (jax.nn.silu(c) @ w_mod[l] + b_mod[l])[:, None, :]
        lw = (w_in[l], w_out[l], g_pre_mix[l], g_post_mix[l], g_pre_ffn[l], g_post_ffn[l], gm_ws[l], gm_bs[l],
              dn_conv[l], dn_a_log[l], dn_dt_bias[l], dn_norm[l], peer_wq[l], peer_keys[l], peer_u[l], peer_v[l])
        s0_ctx = jnp.zeros((xp.shape[0], DN_DIRS, DN_HEADS, DN_HEAD_K, DN_HEAD_V), jnp.float32)
        xp, s_ctx = trunk_layer(xp, mod_ctx, s0_ctx, False, *lw)
        xs, _ = trunk_layer(xs, mod_lat, state_delta[:, l], True, *lw)
        new_states.append(s_ctx.astype(x_prompt.dtype))
    new_state_delta = jnp.stack(new_states, axis=1)
    return (xp, xs, new_state_delta)
```

```python
import functools
import math

import jax
import jax.numpy as jnp
from jax import lax
from jax.experimental import pallas as pl
from jax.experimental.pallas import tpu as pltpu

F32 = jnp.float32
BF16 = jnp.bfloat16
HI = lax.Precision.HIGHEST
EPS = 1e-6
NEG_INF = float("-inf")

D = 1024
GM_HEADS = 4
DN_HEADS = 4
HEAD = 128
CHUNK = 64
GRID_W = 64
ROPE_BASE = 10000.0
PEER_HEADS = 8
NKEYS = 128
TOPK = 16

TM_IN = 512
TM_PEER = 1024
EC_PEER = 1024
TL_TOPK = 256
VMEM_LIMIT = 56 * 1024 * 1024

_NT = (((1,), (1,)), ((), ()))
_TN = (((0,), (0,)), ((), ()))


def _dot(a, b, precision=None):
    return jnp.dot(a, b, preferred_element_type=F32, precision=precision)


def _silu(x):
    return x * jax.nn.sigmoid(x)


def _rms(x, gain):
    return x * lax.rsqrt(jnp.mean(x * x, axis=-1, keepdims=True) + EPS) * gain


def _params(sem):
    return pltpu.CompilerParams(dimension_semantics=sem, vmem_limit_bytes=VMEM_LIMIT)


def _mod_kernel(c_ref, w_ref, b_ref, o_ref):
    o_ref[...] = _dot(_silu(c_ref[...]), w_ref[...], HI) + b_ref[...]


def _mod_call(cc, w_mod, b_mod):
    n = w_mod.shape[1]
    bn = 1536
    return pl.pallas_call(
        _mod_kernel,
        grid=(n // bn,),
        in_specs=[pl.BlockSpec(cc.shape, lambda j: (0, 0)),
                  pl.BlockSpec((D, bn), lambda j: (0, j)),
                  pl.BlockSpec((1, bn), lambda j: (0, j))],
        out_specs=pl.BlockSpec((cc.shape[0], bn), lambda j: (0, j)),
        out_shape=jax.ShapeDtypeStruct((cc.shape[0], n), F32),
        compiler_params=_params(("arbitrary",)),
        name="mod",
    )(cc, w_mod, b_mod)


def _in_kernel(x_ref, mod_ref, gpre_ref, wgu_ref, wgv_ref, wqkv_ref, wz_ref, wab_ref, ws_ref, bs_ref,
               alog_ref, dtb_ref, oa_ref, qkv_ref, zs_ref, gates_ref):
    x = x_ref[...]
    mod = mod_ref[0]
    shift1 = mod[:, 0:D]
    scale1 = mod[:, D:2 * D]
    hb = (_rms(x, gpre_ref[...]) * (1.0 + scale1) + shift1).astype(BF16)
    gu = jax.nn.gelu(_dot(hb, wgu_ref[...]))
    gv = jax.nn.gelu(_dot(hb, wgv_ref[...]))
    qkv_ref[...] = _dot(hb, wqkv_ref[...])
    zs_ref[...] = _silu(_dot(hb, wz_ref[...]))
    pab = _dot(hb, wab_ref[...])
    t = pab + dtb_ref[...]
    softplus = jnp.maximum(t, 0.0) + jnp.log(1.0 + jnp.exp(-jnp.abs(t)))
    lane = lax.broadcasted_iota(jnp.int32, pab.shape, 1)
    gates_ref[...] = jnp.where(lane < 2 * DN_HEADS, -jnp.exp(alog_ref[...]) * softplus, jax.nn.sigmoid(pab))
    for hd in range(GM_HEADS):
        cols = slice(hd * HEAD, (hd + 1) * HEAD)
        v_h = gv[:, cols]
        vn = (v_h * lax.rsqrt(jnp.mean(v_h * v_h, axis=-1, keepdims=True) + EPS)).astype(BF16)
        for c in range(x.shape[0] // HEAD):
            rows = slice(c * HEAD, (c + 1) * HEAD)
            s = _dot(ws_ref[hd], vn[rows]) + bs_ref[hd]
            oa_ref[rows, cols] = (gu[rows, cols] * s).astype(BF16)


def _in_call(x2, mod3, mod_row, gpre, wgu, wgv, wqkv, wz, wab, ws, bs, alog, dtb):
    t = x2.shape[0]
    tm = TM_IN
    full = lambda a: pl.BlockSpec(a.shape, lambda i: (0,) * a.ndim)
    return pl.pallas_call(
        _in_kernel,
        grid=(t // tm,),
        in_specs=[pl.BlockSpec((tm, D), lambda i: (i, 0)),
                  pl.BlockSpec((1, 1, 6 * D), lambda i: (mod_row(i, tm), 0, 0)),
                  full(gpre), full(wgu), full(wgv), full(wqkv), full(wz), full(wab), full(ws), full(bs),
                  full(alog), full(dtb)],
        out_specs=[pl.BlockSpec((tm, 512), lambda i: (i, 0)),
                   pl.BlockSpec((tm, 1536), lambda i: (i, 0)),
                   pl.BlockSpec((tm, 512), lambda i: (i, 0)),
                   pl.BlockSpec((tm, HEAD), lambda i: (i, 0))],
        out_shape=[jax.ShapeDtypeStruct((t, 512), BF16),
                   jax.ShapeDtypeStruct((t, 1536), F32),
                   jax.ShapeDtypeStruct((t, 512), F32),
                   jax.ShapeDtypeStruct((t, HEAD), F32)],
        compiler_params=_params(("parallel",)),
        name="in_proj",
    )(x2, mod3, gpre, wgu, wgv, wqkv, wz, wab, ws, bs, alog, dtb)


def _dn_kernel(*refs, seq, rope, has_s0, want_state):
    it = iter(refs)
    q_ref, k_ref, v_ref, gates_ref, zs_ref = next(it), next(it), next(it), next(it), next(it)
    cwq_ref, cwk_ref, cwv_ref, nw_ref = next(it), next(it), next(it), next(it)
    cos_ref = next(it) if rope else None
    sin_ref = next(it) if rope else None
    s0_ref = next(it) if has_s0 else None
    ob_ref = next(it)
    st_ref = next(it) if want_state else None
    (q_s, k_s, v_s, gf_s, gb_s, bf_s, bb_s, wqf_s, wqb_s, uf_s, ub_s, kdf_s, kdb_s, at_s, glf_s, glb_s,
     of_s, obk_s) = it
    n_chunks = seq // CHUNK
    hd = pl.program_id(1)

    row = lax.broadcasted_iota(jnp.int32, (seq, HEAD), 0)
    lane = lax.broadcasted_iota(jnp.int32, (seq, HEAD), 1)

    def conv_silu(ref, cw_ref):
        x = ref[0]
        w = cw_ref[...]
        xm = jnp.where(row == 0, 0.0, pltpu.roll(x, 1, axis=0))
        xp = jnp.where(row == seq - 1, 0.0, pltpu.roll(x, seq - 1, axis=0))
        return _silu(xm * w[0:1] + x * w[1:2] + xp * w[2:3])

    def l2n(x):
        return x * lax.rsqrt(jnp.sum(x * x, axis=-1, keepdims=True) + EPS)

    def rot(x):
        if not rope:
            return x
        partner = jnp.where((lane & 32) == 0, pltpu.roll(x, 96, axis=1), pltpu.roll(x, 32, axis=1))
        return x * cos_ref[...] + partner * sin_ref[...]

    q_s[...] = rot(l2n(conv_silu(q_ref, cwq_ref))) * (HEAD ** -0.5)
    k_s[...] = rot(l2n(conv_silu(k_ref, cwk_ref)))
    v_s[...] = conv_silu(v_ref, cwv_ref)

    gates = gates_ref[0]
    sel_r = lax.broadcasted_iota(jnp.int32, (HEAD, HEAD), 0)
    for dst, col in ((gf_s, hd), (gb_s, DN_HEADS + hd), (bf_s, 2 * DN_HEADS + hd), (bb_s, 3 * DN_HEADS + hd)):
        dst[...] = _dot(gates, (sel_r == col).astype(F32), HI)

    ri = lax.broadcasted_iota(jnp.int32, (CHUNK, HEAD), 0)
    li = lax.broadcasted_iota(jnp.int32, (CHUNK, HEAD), 1)
    lj = li & (CHUNK - 1)
    is_f = li < CHUNK
    is_b = li >= CHUNK
    incl = (is_f & (ri >= lj)) | (is_b & (ri <= lj))
    strict = (is_f & (ri > lj)) | (is_b & (ri < lj))
    diag = lj == ri
    ti = lax.broadcasted_iota(jnp.int32, (CHUNK, CHUNK), 0)
    tj = lax.broadcasted_iota(jnp.int32, (CHUNK, CHUNK), 1)
    lower = (tj <= ti).astype(F32)
    upper = (tj >= ti).astype(F32)
    ones = jnp.ones((CHUNK, CHUNK), F32)
    ei = lax.broadcasted_iota(jnp.int32, (2 * CHUNK, 2 * CHUNK), 0)
    ej = lax.broadcasted_iota(jnp.int32, (2 * CHUNK, 2 * CHUNK), 1)
    eye = (ei == ej).astype(F32)

    def prep(c, carry):
        r0 = pl.multiple_of(c * CHUNK, CHUNK)
        rows = pl.ds(r0, CHUNK)
        qc, kc, vc = q_s[rows, :], k_s[rows, :], v_s[rows, :]
        gfc, gbc, bfc, bbc = gf_s[rows, :], gb_s[rows, :], bf_s[rows, :], bb_s[rows, :]
        gc_f = _dot(lower, gfc, HI)
        gc_b = _dot(upper, gbc, HI)
        gc = jnp.where(is_f, gc_f, gc_b)
        gc_row = _dot(ones, jnp.where(diag, gc, 0.0), HI)
        dec = jnp.exp(jnp.where(incl, gc - gc_row, NEG_INF))
        kk2 = jnp.concatenate([kc, kc], axis=0)
        kk = lax.dot_general(kc, kk2, _NT, precision=HI, preferred_element_type=F32)
        qk = lax.dot_general(qc, kk2, _NT, precision=HI, preferred_element_type=F32)
        beta = jnp.where(is_f, bfc, bbc)
        a = jnp.where(strict, beta * kk * dec, 0.0)
        x = -jnp.concatenate([jnp.where(is_f, a, 0.0), jnp.where(is_f, 0.0, a)], axis=0)
        p = eye + x
        for _ in range(5):
            x = _dot(x, x, HI)
            p = p + _dot(p, x, HI)
        rhs = jnp.concatenate(
            [jnp.concatenate([kc * bfc * jnp.exp(gc_f), vc * bfc], axis=1),
             jnp.concatenate([kc * bbc * jnp.exp(gc_b), vc * bbc], axis=1)], axis=0)
        wu = _dot(p, rhs, HI)
        tot_f = jnp.sum(gfc, axis=0, keepdims=True)
        tot_b = jnp.sum(gbc, axis=0, keepdims=True)
        r2 = pl.multiple_of(c * 2 * CHUNK, 2 * CHUNK)
        wqf_s[pl.ds(r2, 2 * CHUNK), :] = jnp.concatenate([wu[:CHUNK, :HEAD], qc * jnp.exp(gc_f)], axis=0)
        wqb_s[pl.ds(r2, 2 * CHUNK), :] = jnp.concatenate([wu[CHUNK:, :HEAD], qc * jnp.exp(gc_b)], axis=0)
        uf_s[rows, :] = wu[:CHUNK, HEAD:]
        ub_s[rows, :] = wu[CHUNK:, HEAD:]
        kdf_s[rows, :] = kc * jnp.exp(tot_f - gc_f)
        kdb_s[rows, :] = kc * jnp.exp(tot_b - gc_b)
        at_s[rows, :] = qk * dec
        r8 = pl.multiple_of(c * 8, 8)
        glf_s[pl.ds(r8, 8), :] = jnp.broadcast_to(jnp.exp(tot_f), (8, HEAD))
        glb_s[pl.ds(r8, 8), :] = jnp.broadcast_to(jnp.exp(tot_b), (8, HEAD))
        return carry

    lax.fori_loop(0, n_chunks, prep, 0)

    def one_dir(c, s, wq_s, u_s, kd_s, gl_s, backward, o_s):
        rows = pl.ds(pl.multiple_of(c * CHUNK, CHUNK), CHUNK)
        wq = wq_s[pl.ds(pl.multiple_of(c * 2 * CHUNK, 2 * CHUNK), 2 * CHUNK), :]
        r = _dot(wq, s, HI)
        v_new = u_s[rows, :] - r[:CHUNK]
        at = at_s[rows, :]
        if backward:
            at = pltpu.roll(at, CHUNK, axis=1)
        o_s[rows, :] = r[CHUNK:] + _dot(at[:, :CHUNK], v_new, HI)
        gl = gl_s[pl.ds(pl.multiple_of(c * 8, 8), 1), :]
        return s * gl + lax.dot_general(kd_s[rows, :], v_new, _TN, precision=HI, preferred_element_type=F32)

    def scan(i, carry):
        s_f, s_b = carry
        s_f = one_dir(i, s_f, wqf_s, uf_s, kdf_s, glf_s, False, of_s)
        s_b = one_dir(n_chunks - 1 - i, s_b, wqb_s, ub_s, kdb_s, glb_s, True, obk_s)
        return s_f, s_b

    if has_s0:
        init = (s0_ref[0, 0, 0], s0_ref[0, 1, 0])
    else:
        init = (jnp.zeros((HEAD, HEAD), F32), jnp.zeros((HEAD, HEAD), F32))
    s_f, s_b = lax.fori_loop(0, n_chunks, scan, init)
    if want_state:
        st_ref[0, 0, 0] = s_f
        st_ref[0, 1, 0] = s_b

    o = of_s[...] + obk_s[...]
    ob_ref[0] = (_rms(o, nw_ref[...]) * zs_ref[0]).astype(BF16)


def _dn_call(qkv3, gates3, zs3, conv_w, norm_w, cos, sin, s0, want_state):
    b, seq, _ = qkv3.shape
    rope = cos is not None
    has_s0 = s0 is not None
    n_chunks = seq // CHUNK
    col = lambda off: pl.BlockSpec((1, seq, HEAD), lambda bi, h: (bi, 0, off + h))
    cw = lambda off: pl.BlockSpec((3, HEAD), lambda bi, h: (0, off + h))
    in_specs = [col(0), col(DN_HEADS), col(2 * DN_HEADS),
                pl.BlockSpec((1, seq, HEAD), lambda bi, h: (bi, 0, 0)),
                pl.BlockSpec((1, seq, HEAD), lambda bi, h: (bi, 0, h)),
                cw(0), cw(DN_HEADS), cw(2 * DN_HEADS),
                pl.BlockSpec((1, HEAD), lambda bi, h: (0, 0))]
    args = [qkv3, qkv3, qkv3, gates3, zs3, conv_w, conv_w, conv_w, norm_w]
    if rope:
        in_specs += [pl.BlockSpec((seq, HEAD), lambda bi, h: (0, 0))] * 2
        args += [cos, sin]
    st_spec = pl.BlockSpec((1, 2, 1, HEAD, HEAD), lambda bi, h: (bi, 0, h, 0, 0))
    if has_s0:
        in_specs.append(st_spec)
        args.append(s0)
    out_specs = [pl.BlockSpec((1, seq, HEAD), lambda bi, h: (bi, 0, h))]
    out_shape = [jax.ShapeDtypeStruct((b, seq, DN_HEADS * HEAD), BF16)]
    if want_state:
        out_specs.append(st_spec)
        out_shape.append(jax.ShapeDtypeStruct((b, 2, DN_HEADS, HEAD, HEAD), F32))
    big = pltpu.VMEM((seq, HEAD), F32)
    scratch = [big, big, big, big, big, big, big,
               pltpu.VMEM((2 * seq, HEAD), F32), pltpu.VMEM((2 * seq, HEAD), F32),
               big, big, big, big, big,
               pltpu.VMEM((8 * n_chunks, HEAD), F32), pltpu.VMEM((8 * n_chunks, HEAD), F32),
               big, big]
    res = pl.pallas_call(
        functools.partial(_dn_kernel, seq=seq, rope=rope, has_s0=has_s0, want_state=want_state),
        grid=(b, DN_HEADS),
        in_specs=in_specs,
        out_specs=out_specs,
        out_shape=out_shape,
        scratch_shapes=scratch,
        compiler_params=_params(("parallel", "parallel")),
        name="deltanet",
    )(*args)
    return res if want_state else (res[0], None)


def _out_kernel(oa_ref, ob_ref, x_ref, mod_ref, wo1_ref, wo2_ref, gpm_ref, gpf_ref, wq_ref, keys_ref,
                x1_ref, h2_ref, st_ref):
    mod = mod_ref[0]
    gate1 = mod[:, 2 * D:3 * D]
    shift2 = mod[:, 3 * D:4 * D]
    scale2 = mod[:, 4 * D:5 * D]
    y = _dot(oa_ref[...], wo1_ref[...]) + _dot(ob_ref[...], wo2_ref[...])
    x1 = x_ref[...] + gate1 * _rms(y, gpm_ref[...])
    x1_ref[...] = x1
    h2 = (_rms(x1, gpf_ref[...]) * (1.0 + scale2) + shift2).astype(BF16)
    h2_ref[...] = h2
    q = _dot(h2, wq_ref[...]).astype(BF16)
    for hp in range(2 * PEER_HEADS):
        st_ref[hp] = lax.dot_general(keys_ref[hp], q[:, hp * HEAD:(hp + 1) * HEAD], _NT,
                                     preferred_element_type=F32)


def _out_call(oa, ob, x2, mod3, mod_row, wo1, wo2, gpm, gpf, wq, keys):
    t = x2.shape[0]
    tm = TM_IN
    full = lambda a: pl.BlockSpec(a.shape, lambda i: (0,) * a.ndim)
    return pl.pallas_call(
        _out_kernel,
        grid=(t // tm,),
        in_specs=[pl.BlockSpec((tm, 512), lambda i: (i, 0)),
                  pl.BlockSpec((tm, 512), lambda i: (i, 0)),
                  pl.BlockSpec((tm, D), lambda i: (i, 0)),
                  pl.BlockSpec((1, 1, 6 * D), lambda i: (mod_row(i, tm), 0, 0)),
                  full(wo1), full(wo2), full(gpm), full(gpf), full(wq), full(keys)],
        out_specs=[pl.BlockSpec((tm, D), lambda i: (i, 0)),
                   pl.BlockSpec((tm, D), lambda i: (i, 0)),
                   pl.BlockSpec((2 * PEER_HEADS, NKEYS, tm), lambda i: (0, 0, i))],
        out_shape=[jax.ShapeDtypeStruct((t, D), F32),
                   jax.ShapeDtypeStruct((t, D), BF16),
                   jax.ShapeDtypeStruct((2 * PEER_HEADS, NKEYS, t), F32)],
        compiler_params=_params(("parallel",)),
        name="out_proj",
    )(oa, ob, x2, mod3, wo1, wo2, gpm, gpf, wq, keys)


def _top16(x):
    vals = []
    for r in range(TOPK):
        m = jnp.max(x, axis=0, keepdims=True)
        vals.append(m)
        if r + 1 < TOPK:
            x = jnp.where(x == m, NEG_INF, x)
    return vals


def _topk_kernel(s_ref, l1_ref, e1_ref, r2_ref, e2_ref):
    def head(h, carry):
        s1 = s_ref[2 * h]
        s2 = s_ref[2 * h + 1]
        a = _top16(s1)
        b = _top16(s2)
        bmat = jnp.concatenate(b, axis=0)
        cands = [a[i] + bmat for i in range(TOPK)]
        work = cands
        thr = None
        for r in range(TOPK):
            m = functools.reduce(jnp.maximum, work)
            thr = jnp.max(m, axis=0, keepdims=True)
            if r + 1 < TOPK:
                work = [jnp.where(w == thr, NEG_INF, w) for w in work]
        m0 = a[0] + b[0]
        z = functools.reduce(
            jnp.add, [jnp.sum(jnp.where(cd >= thr, jnp.exp(cd - m0), 0.0), axis=0, keepdims=True) for cd in cands])
        cnt = jnp.zeros_like(s1)
        rank = jnp.zeros_like(s2)
        for j in range(TOPK):
            cnt = cnt + jnp.where(s1 + b[j] >= thr, 1.0, 0.0)
            rank = rank + jnp.where(b[j] > s2, 1.0, 0.0)
        l1_ref[h] = cnt
        r2_ref[h] = rank
        e1_ref[h] = jnp.exp(s1 - a[0]) / z
        e2_ref[h] = jnp.exp(s2 - b[0])
        return carry

    lax.fori_loop(0, PEER_HEADS, head, 0)


def _topk_call(st):
    t = st.shape[-1]
    tl = TL_TOPK
    spec = pl.BlockSpec((PEER_HEADS, NKEYS, tl), lambda i: (0, 0, i))
    shape = jax.ShapeDtypeStruct((PEER_HEADS, NKEYS, t), F32)
    return pl.pallas_call(
        _topk_kernel,
        grid=(t // tl,),
        in_specs=[pl.BlockSpec((2 * PEER_HEADS, NKEYS, tl), lambda i: (0, 0, i))],
        out_specs=[spec] * 4,
        out_shape=[shape] * 4,
        compiler_params=_params(("parallel",)),
        name="peer_select",
    )(st)


def _peer_kernel(h2_ref, u_ref, vt_ref, l1_ref, e1_ref, r2_ref, e2_ref, acc_ref, at_ref, ct_ref):
    j = pl.program_id(1)
    tm = h2_ref.shape[0]
    ec = u_ref.shape[0]
    n_lb = tm // HEAD

    @pl.when(j == 0)
    def _():
        acc_ref[...] = jnp.zeros_like(acc_ref)

    at_ref[...] = lax.dot_general(u_ref[...], h2_ref[...], _NT, preferred_element_type=F32)

    half = NKEYS // 2

    def body(lb, carry):
        cols = pl.ds(pl.multiple_of(lb * HEAD, HEAD), HEAD)
        for i1 in range(ec // NKEYS):
            for part in range(2):
                keys2 = slice(part * half, (part + 1) * half)
                g = jnp.zeros((half, HEAD), F32)
                for h in range(PEER_HEADS):
                    l1 = l1_ref[h, i1:i1 + 1, cols]
                    e1 = e1_ref[h, i1:i1 + 1, cols]
                    g = g + jnp.where(r2_ref[h, keys2, cols] < l1, e2_ref[h, keys2, cols], 0.0) * e1
                rows = slice(i1 * NKEYS + part * half, i1 * NKEYS + (part + 1) * half)
                ct_ref[rows, cols] = (g * jax.nn.gelu(at_ref[rows, cols])).astype(BF16)
        return carry

    lax.fori_loop(0, n_lb, body, 0)
    acc_ref[...] += _dot(vt_ref[...], ct_ref[...])


def _peer_call(h2, u_bf, vt_bf, l1, e1, r2, e2):
    t = h2.shape[0]
    tm, ec = TM_PEER, EC_PEER
    n_exp = u_bf.shape[0]
    i1_blk = ec // NKEYS
    return pl.pallas_call(
        _peer_kernel,
        grid=(t // tm, n_exp // ec),
        in_specs=[pl.BlockSpec((tm, D), lambda i, j: (i, 0)),
                  pl.BlockSpec((ec, D), lambda i, j: (j, 0)),
                  pl.BlockSpec((D, ec), lambda i, j: (0, j)),
                  pl.BlockSpec((PEER_HEADS, i1_blk, tm), lambda i, j: (0, j, i)),
                  pl.BlockSpec((PEER_HEADS, i1_blk, tm), lambda i, j: (0, j, i)),
                  pl.BlockSpec((PEER_HEADS, NKEYS, tm), lambda i, j: (0, 0, i)),
                  pl.BlockSpec((PEER_HEADS, NKEYS, tm), lambda i, j: (0, 0, i))],
        out_specs=pl.BlockSpec((D, tm), lambda i, j: (0, i)),
        out_shape=jax.ShapeDtypeStruct((D, t), F32),
        scratch_shapes=[pltpu.VMEM((ec, tm), F32), pltpu.VMEM((ec, tm), BF16)],
        compiler_params=_params(("parallel", "arbitrary")),
        name="peer",
    )(h2, u_bf, vt_bf, l1, e1, r2, e2)


def _fin_kernel(ot_ref, x1_ref, mod_ref, gpost_ref, y_ref):
    gate2 = mod_ref[0][:, 5 * D:6 * D]
    y_ref[...] = x1_ref[...] + gate2 * _rms(ot_ref[...].T, gpost_ref[...])


def _fin_call(ot, x1, mod3, mod_row, gpost):
    t = x1.shape[0]
    tm = TM_IN
    return pl.pallas_call(
        _fin_kernel,
        grid=(t // tm,),
        in_specs=[pl.BlockSpec((D, tm), lambda i: (0, i)),
                  pl.BlockSpec((tm, D), lambda i: (i, 0)),
                  pl.BlockSpec((1, 1, 6 * D), lambda i: (mod_row(i, tm), 0, 0)),
                  pl.BlockSpec((1, D), lambda i: (0, 0))],
        out_specs=pl.BlockSpec((tm, D), lambda i: (i, 0)),
        out_shape=jax.ShapeDtypeStruct((t, D), F32),
        compiler_params=_params(("parallel",)),
        name="peer_residual",
    )(ot, x1, mod3, gpost)


def _rope_tables(seq):
    pos = jnp.arange(seq)
    quarter = HEAD // 4
    inv = ROPE_BASE ** (-jnp.arange(quarter, dtype=F32) / quarter)
    ang_r = (pos // GRID_W).astype(F32)[:, None] * inv
    ang_c = (pos % GRID_W).astype(F32)[:, None] * inv
    cos = jnp.concatenate([jnp.cos(ang_r)] * 2 + [jnp.cos(ang_c)] * 2, axis=-1)
    sin = jnp.concatenate([-jnp.sin(ang_r), jnp.sin(ang_r), -jnp.sin(ang_c), jnp.sin(ang_c)], axis=-1)
    return cos, sin


def _layer(x, mod3, mod_row, s0, rope, want_state, w):
    b, seq, _ = x.shape
    x2 = x.reshape(b * seq, D)
    oa, qkv, zs, gates = _in_call(x2, mod3, mod_row, w["gpre"], w["wgu"], w["wgv"], w["wqkv"], w["wz"], w["wab"],
                                  w["ws"], w["bs"], w["alog"], w["dtb"])
    cos, sin = _rope_tables(seq) if rope else (None, None)
    ob, st = _dn_call(qkv.reshape(b, seq, -1), gates.reshape(b, seq, HEAD), zs.reshape(b, seq, -1),
                      w["conv"], w["dnorm"], cos, sin, s0, want_state)
    x1, h2, sc = _out_call(oa, ob.reshape(b * seq, -1), x2, mod3, mod_row, w["wo1"], w["wo2"], w["gpm"], w["gpf"],
                           w["wq"], w["keys"])
    l1, e1, r2, e2 = _topk_call(sc)
    ot = _peer_call(h2, w["u"], w["vt"], l1, e1, r2, e2)
    y = _fin_call(ot, x1, mod3, mod_row, w["gpost"])
    return y.reshape(b, seq, D), st


def kernel(x_prompt, x_sample, state_delta, c, c_ctx, w_mod, b_mod, g_pre_mix, g_post_mix, g_pre_ffn, g_post_ffn,
           w_in, w_out, gm_ws, gm_bs, dn_conv, dn_a_log, dn_dt_bias, dn_norm, peer_wq, peer_keys, peer_u, peer_v):
    depth = w_mod.shape[0]
    dec_batch, dec_seq, _ = x_sample.shape
    xp, xs = x_prompt, x_sample
    states = []
    cc = jnp.zeros((16, D), F32).at[0].set(c_ctx).at[1:1 + dec_batch].set(c)
    for l in range(depth):
        mod3 = _mod_call(cc, w_mod[l], b_mod[l][None, :]).reshape(16, 1, 6 * D)
        wi = w_in[l]
        pad16 = lambda v: jnp.zeros((1, HEAD), F32).at[0, :v.size].set(v.reshape(-1))
        w = {
            "gpre": g_pre_mix[l][None, :], "gpm": g_post_mix[l][None, :], "gpf": g_pre_ffn[l][None, :],
            "gpost": g_post_ffn[l][None, :],
            "wgu": wi[:, 0:512].astype(BF16), "wgv": wi[:, 512:1024].astype(BF16),
            "wqkv": wi[:, 1024:2560].astype(BF16), "wz": wi[:, 2560:3072].astype(BF16),
            "wab": jnp.zeros((D, HEAD), F32).at[:, :16].set(wi[:, 3072:3088]).astype(BF16),
            "ws": gm_ws[l].astype(BF16),
            "bs": jnp.broadcast_to(gm_bs[l][:, :, None], (GM_HEADS, HEAD, HEAD)),
            "alog": pad16(dn_a_log[l]), "dtb": pad16(dn_dt_bias[l]),
            "conv": dn_conv[l], "dnorm": dn_norm[l][None, :],
            "wo1": w_out[l][:512].astype(BF16), "wo2": w_out[l][512:].astype(BF16),
            "wq": peer_wq[l].astype(BF16),
            "keys": peer_keys[l].reshape(2 * PEER_HEADS, NKEYS, HEAD).astype(BF16),
            "u": peer_u[l].astype(BF16), "vt": peer_v[l].T.astype(BF16),
        }
        xp, s_ctx = _layer(xp, mod3, lambda i, tm: 0, None, False, True, w)
        xs, _ = _layer(xs, mod3, lambda i, tm: 1 + (i * tm) // dec_seq, state_delta[:, l], True, False, w)
        states.append(s_ctx.astype(x_prompt.dtype))
    return xp, xs, jnp.stack(states, axis=1)
```

```python
import functools
import math

import jax
import jax.numpy as jnp
from jax import lax
from jax.experimental import pallas as pl
from jax.experimental.pallas import tpu as pltpu

F32 = jnp.float32
BF16 = jnp.bfloat16
HI = lax.Precision.HIGHEST
EPS = 1e-6
NEG_INF = float("-inf")

D = 1024
GM_HEADS = 4
DN_HEADS = 4
HEAD = 128
CHUNK = 64
GRID_W = 64
ROPE_BASE = 10000.0
PEER_HEADS = 8
NKEYS = 128
TOPK = 16

TM_IN = 512
TM_PEER = 1024
EC_PEER = 1024
TB_PEER = 512
TL_TOPK = 256
VMEM_LIMIT = 56 * 1024 * 1024

_NT = (((1,), (1,)), ((), ()))
_TN = (((0,), (0,)), ((), ()))


def _dot(a, b, precision=None):
    return jnp.dot(a, b, preferred_element_type=F32, precision=precision)


def _pack_rows(x):
    r, c = x.shape
    u = lax.bitcast_convert_type(x, jnp.uint16).astype(jnp.uint32).reshape(r // 2, 2, c)
    return u[:, 0] | (u[:, 1] << 16)


def _split(x):
    hi = x.astype(BF16)
    return hi, (x - hi.astype(F32)).astype(BF16)


def _dot_split_rhs(m, x):
    hi, lo = _split(x)
    return _dot(m, hi) + _dot(m, lo)


def _dot_split_lhs(x, m):
    hi, lo = _split(x)
    return _dot(hi, m) + _dot(lo, m)


def _dot3(a, b):
    ah, al = _split(a)
    bh, bl = _split(b)
    return _dot(ah, bh) + (_dot(ah, bl) + _dot(al, bh))


_GELU_C1 = math.sqrt(2.0 / math.pi)
_GELU_C2 = _GELU_C1 * 0.044715


def _gelu(x):
    hx = 0.5 * x
    return hx + hx * jnp.tanh(x * (_GELU_C1 + _GELU_C2 * (x * x)))


def _silu(x):
    return x * jax.nn.sigmoid(x)


def _rms(x, gain):
    return x * lax.rsqrt(jnp.mean(x * x, axis=-1, keepdims=True) + EPS) * gain


def _params(sem):
    return pltpu.CompilerParams(dimension_semantics=sem, vmem_limit_bytes=VMEM_LIMIT)


def _mod_kernel(c_ref, w_ref, b_ref, o_ref):
    o_ref[...] = _dot(_silu(c_ref[...]), w_ref[...], HI) + b_ref[...]


def _mod_call(cc, w_mod, b_mod):
    n = w_mod.shape[1]
    bn = 1536
    return pl.pallas_call(
        _mod_kernel,
        grid=(n // bn,),
        in_specs=[pl.BlockSpec(cc.shape, lambda j: (0, 0)),
                  pl.BlockSpec((D, bn), lambda j: (0, j)),
                  pl.BlockSpec((1, bn), lambda j: (0, j))],
        out_specs=pl.BlockSpec((cc.shape[0], bn), lambda j: (0, j)),
        out_shape=jax.ShapeDtypeStruct((cc.shape[0], n), F32),
        compiler_params=_params(("arbitrary",)),
        name="mod",
    )(cc, w_mod, b_mod)


def _in_kernel(x_ref, mod_ref, gpre_ref, wgu_ref, wgv_ref, wqkv_ref, wz_ref, wab_ref, ws_ref, bs_ref,
               alog_ref, dtb_ref, oa_ref, qkv_ref, zs_ref, gates_ref):
    x = x_ref[...]
    mod = mod_ref[0]
    shift1 = mod[:, 0:D]
    scale1 = mod[:, D:2 * D]
    hb = (_rms(x, gpre_ref[...]) * (1.0 + scale1) + shift1).astype(BF16)
    gu = _gelu(_dot(hb, wgu_ref[...]))
    gv = _gelu(_dot(hb, wgv_ref[...]))
    qkv_ref[...] = _dot(hb, wqkv_ref[...])
    zs_ref[...] = _silu(_dot(hb, wz_ref[...]))
    pab = _dot(hb, wab_ref[...])
    t = pab + dtb_ref[...]
    softplus = jnp.maximum(t, 0.0) + jnp.log(1.0 + jnp.exp(-jnp.abs(t)))
    lane = lax.broadcasted_iota(jnp.int32, pab.shape, 1)
    gates_ref[...] = jnp.where(lane < 2 * DN_HEADS, -jnp.exp(alog_ref[...]) * softplus, jax.nn.sigmoid(pab))
    for hd in range(GM_HEADS):
        cols = slice(hd * HEAD, (hd + 1) * HEAD)
        v_h = gv[:, cols]
        vn = (v_h * lax.rsqrt(jnp.mean(v_h * v_h, axis=-1, keepdims=True) + EPS)).astype(BF16)
        for c in range(x.shape[0] // HEAD):
            rows = slice(c * HEAD, (c + 1) * HEAD)
            s = _dot(ws_ref[hd], vn[rows]) + bs_ref[hd]
            oa_ref[rows, cols] = (gu[rows, cols] * s).astype(BF16)


def _in_call(x2, mod3, mod_row, gpre, wgu, wgv, wqkv, wz, wab, ws, bs, alog, dtb):
    t = x2.shape[0]
    tm = TM_IN
    full = lambda a: pl.BlockSpec(a.shape, lambda i: (0,) * a.ndim)
    return pl.pallas_call(
        _in_kernel,
        grid=(t // tm,),
        in_specs=[pl.BlockSpec((tm, D), lambda i: (i, 0)),
                  pl.BlockSpec((1, 1, 6 * D), lambda i: (mod_row(i, tm), 0, 0)),
                  full(gpre), full(wgu), full(wgv), full(wqkv), full(wz), full(wab), full(ws), full(bs),
                  full(alog), full(dtb)],
        out_specs=[pl.BlockSpec((tm, 512), lambda i: (i, 0)),
                   pl.BlockSpec((tm, 1536), lambda i: (i, 0)),
                   pl.BlockSpec((tm, 512), lambda i: (i, 0)),
                   pl.BlockSpec((tm, HEAD), lambda i: (i, 0))],
        out_shape=[jax.ShapeDtypeStruct((t, 512), BF16),
                   jax.ShapeDtypeStruct((t, 1536), F32),
                   jax.ShapeDtypeStruct((t, 512), F32),
                   jax.ShapeDtypeStruct((t, HEAD), F32)],
        compiler_params=_params(("parallel",)),
        name="in_proj",
    )(x2, mod3, gpre, wgu, wgv, wqkv, wz, wab, ws, bs, alog, dtb)


def _dn_kernel(*refs, seq, rope, has_s0, want_state):
    it = iter(refs)
    q_ref, k_ref, v_ref, gates_ref, zs_ref = next(it), next(it), next(it), next(it), next(it)
    cwq_ref, cwk_ref, cwv_ref, nw_ref = next(it), next(it), next(it), next(it)
    cos_ref = next(it) if rope else None
    sin_ref = next(it) if rope else None
    s0_ref = next(it) if has_s0 else None
    ob_ref = next(it)
    st_ref = next(it) if want_state else None
    (q_s, k_s, v_s, gf_s, gb_s, bf_s, bb_s, wqf_s, wqb_s, uf_s, ub_s, kdf_s, kdb_s, at_s, glf_s, glb_s,
     of_s, obk_s) = it
    n_chunks = seq // CHUNK
    hd = pl.program_id(1)

    row = lax.broadcasted_iota(jnp.int32, (seq, HEAD), 0)
    lane = lax.broadcasted_iota(jnp.int32, (seq, HEAD), 1)

    def conv_silu(ref, cw_ref):
        x = ref[0]
        w = cw_ref[...]
        xm = jnp.where(row == 0, 0.0, pltpu.roll(x, 1, axis=0))
        xp = jnp.where(row == seq - 1, 0.0, pltpu.roll(x, seq - 1, axis=0))
        return _silu(xm * w[0:1] + x * w[1:2] + xp * w[2:3])

    def l2n(x):
        return x * lax.rsqrt(jnp.sum(x * x, axis=-1, keepdims=True) + EPS)

    def rot(x):
        if not rope:
            return x
        partner = jnp.where((lane & 32) == 0, pltpu.roll(x, 96, axis=1), pltpu.roll(x, 32, axis=1))
        return x * cos_ref[...] + partner * sin_ref[...]

    q_s[...] = rot(l2n(conv_silu(q_ref, cwq_ref))) * (HEAD ** -0.5)
    k_s[...] = rot(l2n(conv_silu(k_ref, cwk_ref)))
    v_s[...] = conv_silu(v_ref, cwv_ref)

    gates = gates_ref[0]
    sel_r = lax.broadcasted_iota(jnp.int32, (HEAD, HEAD), 0)
    for dst, col in ((gf_s, hd), (gb_s, DN_HEADS + hd), (bf_s, 2 * DN_HEADS + hd), (bb_s, 3 * DN_HEADS + hd)):
        dst[...] = _dot_split_lhs(gates, (sel_r == col).astype(BF16))

    ri = lax.broadcasted_iota(jnp.int32, (CHUNK, HEAD), 0)
    li = lax.broadcasted_iota(jnp.int32, (CHUNK, HEAD), 1)
    lj = li & (CHUNK - 1)
    is_f = li < CHUNK
    is_b = li >= CHUNK
    incl = (is_f & (ri >= lj)) | (is_b & (ri <= lj))
    strict = (is_f & (ri > lj)) | (is_b & (ri < lj))
    diag = lj == ri
    ti = lax.broadcasted_iota(jnp.int32, (CHUNK, CHUNK), 0)
    tj = lax.broadcasted_iota(jnp.int32, (CHUNK, CHUNK), 1)
    lower = (tj <= ti).astype(BF16)
    upper = (tj >= ti).astype(BF16)
    ones = jnp.ones((CHUNK, CHUNK), BF16)
    ei = lax.broadcasted_iota(jnp.int32, (2 * CHUNK, 2 * CHUNK), 0)
    ej = lax.broadcasted_iota(jnp.int32, (2 * CHUNK, 2 * CHUNK), 1)
    eye = (ei == ej).astype(F32)

    def prep(c, carry):
        r0 = pl.multiple_of(c * CHUNK, CHUNK)
        rows = pl.ds(r0, CHUNK)
        qc, kc, vc = q_s[rows, :], k_s[rows, :], v_s[rows, :]
        gfc, gbc, bfc, bbc = gf_s[rows, :], gb_s[rows, :], bf_s[rows, :], bb_s[rows, :]
        gc_f = _dot_split_rhs(lower, gfc)
        gc_b = _dot_split_rhs(upper, gbc)
        gc = jnp.where(is_f, gc_f, gc_b)
        gc_row = _dot_split_rhs(ones, jnp.where(diag, gc, 0.0))
        dec = jnp.exp(jnp.where(incl, gc - gc_row, NEG_INF))
        kb = kc.astype(BF16)
        kk2 = jnp.concatenate([kb, kb], axis=0)
        kk = lax.dot_general(kb, kk2, _NT, preferred_element_type=F32)
        qk = lax.dot_general(qc.astype(BF16), kk2, _NT, preferred_element_type=F32)
        beta = jnp.where(is_f, bfc, bbc)
        a = jnp.where(strict, beta * kk * dec, 0.0)
        x = -jnp.concatenate([jnp.where(is_f, a, 0.0), jnp.where(is_f, 0.0, a)], axis=0)
        p = eye + x
        for _ in range(5):
            x = _dot3(x, x)
            p = p + _dot3(p, x)
        rhs = jnp.concatenate(
            [jnp.concatenate([kc * bfc * jnp.exp(gc_f), vc * bfc], axis=1),
             jnp.concatenate([kc * bbc * jnp.exp(gc_b), vc * bbc], axis=1)], axis=0)
        wu = _dot(p.astype(BF16), rhs.astype(BF16))
        tot_f = jnp.sum(gfc, axis=0, keepdims=True)
        tot_b = jnp.sum(gbc, axis=0, keepdims=True)
        r2 = pl.multiple_of(c * 2 * CHUNK, 2 * CHUNK)
        wqf_s[pl.ds(r2, 2 * CHUNK), :] = jnp.concatenate([wu[:CHUNK, :HEAD], qc * jnp.exp(gc_f)], axis=0)
        wqb_s[pl.ds(r2, 2 * CHUNK), :] = jnp.concatenate([wu[CHUNK:, :HEAD], qc * jnp.exp(gc_b)], axis=0)
        uf_s[rows, :] = wu[:CHUNK, HEAD:]
        ub_s[rows, :] = wu[CHUNK:, HEAD:]
        kdf_s[rows, :] = kc * jnp.exp(tot_f - gc_f)
        kdb_s[rows, :] = kc * jnp.exp(tot_b - gc_b)
        at_s[rows, :] = qk * dec
        r8 = pl.multiple_of(c * 8, 8)
        glf_s[pl.ds(r8, 8), :] = jnp.broadcast_to(jnp.exp(tot_f), (8, HEAD))
        glb_s[pl.ds(r8, 8), :] = jnp.broadcast_to(jnp.exp(tot_b), (8, HEAD))
        return carry

    lax.fori_loop(0, n_chunks, prep, 0, unroll=2)

    def one_dir(c, s, wq_s, u_s, kd_s, gl_s, backward, o_s):
        rows = pl.ds(pl.multiple_of(c * CHUNK, CHUNK), CHUNK)
        wq = wq_s[pl.ds(pl.multiple_of(c * 2 * CHUNK, 2 * CHUNK), 2 * CHUNK), :]
        r = _dot(wq.astype(BF16), s.astype(BF16))
        v_new = u_s[rows, :] - r[:CHUNK]
        vb = v_new.astype(BF16)
        at = at_s[rows, :]
        if backward:
            at = pltpu.roll(at, CHUNK, axis=1)
        o_s[rows, :] = r[CHUNK:] + _dot(at[:, :CHUNK].astype(BF16), vb)
        gl = gl_s[pl.ds(pl.multiple_of(c * 8, 8), 1), :]
        return s * gl + lax.dot_general(kd_s[rows, :].astype(BF16), vb, _TN, preferred_element_type=F32)

    def scan(i, carry):
        s_f, s_b = carry
        s_f = one_dir(i, s_f, wqf_s, uf_s, kdf_s, glf_s, False, of_s)
        s_b = one_dir(n_chunks - 1 - i, s_b, wqb_s, ub_s, kdb_s, glb_s, True, obk_s)
        return s_f, s_b

    if has_s0:
        init = (s0_ref[0, 0, 0], s0_ref[0, 1, 0])
    else:
        init = (jnp.zeros((HEAD, HEAD), F32), jnp.zeros((HEAD, HEAD), F32))
    s_f, s_b = lax.fori_loop(0, n_chunks, scan, init)
    if want_state:
        st_ref[0, 0, 0] = s_f
        st_ref[0, 1, 0] = s_b

    o = of_s[...] + obk_s[...]
    ob_ref[0] = (_rms(o, nw_ref[...]) * zs_ref[0]).astype(BF16)


def _dn_call(qkv3, gates3, zs3, conv_w, norm_w, cos, sin, s0, want_state):
    b, seq, _ = qkv3.shape
    rope = cos is not None
    has_s0 = s0 is not None
    n_chunks = seq // CHUNK
    col = lambda off: pl.BlockSpec((1, seq, HEAD), lambda bi, h: (bi, 0, off + h))
    cw = lambda off: pl.BlockSpec((3, HEAD), lambda bi, h: (0, off + h))
    in_specs = [col(0), col(DN_HEADS), col(2 * DN_HEADS),
                pl.BlockSpec((1, seq, HEAD), lambda bi, h: (bi, 0, 0)),
                pl.BlockSpec((1, seq, HEAD), lambda bi, h: (bi, 0, h)),
                cw(0), cw(DN_HEADS), cw(2 * DN_HEADS),
                pl.BlockSpec((1, HEAD), lambda bi, h: (0, 0))]
    args = [qkv3, qkv3, qkv3, gates3, zs3, conv_w, conv_w, conv_w, norm_w]
    if rope:
        in_specs += [pl.BlockSpec((seq, HEAD), lambda bi, h: (0, 0))] * 2
        args += [cos, sin]
    st_spec = pl.BlockSpec((1, 2, 1, HEAD, HEAD), lambda bi, h: (bi, 0, h, 0, 0))
    if has_s0:
        in_specs.append(st_spec)
        args.append(s0)
    out_specs = [pl.BlockSpec((1, seq, HEAD), lambda bi, h: (bi, 0, h))]
    out_shape = [jax.ShapeDtypeStruct((b, seq, DN_HEADS * HEAD), BF16)]
    if want_state:
        out_specs.append(st_spec)
        out_shape.append(jax.ShapeDtypeStruct((b, 2, DN_HEADS, HEAD, HEAD), F32))
    big = pltpu.VMEM((seq, HEAD), F32)
    scratch = [big, big, big, big, big, big, big,
               pltpu.VMEM((2 * seq, HEAD), F32), pltpu.VMEM((2 * seq, HEAD), F32),
               big, big, big, big, big,
               pltpu.VMEM((8 * n_chunks, HEAD), F32), pltpu.VMEM((8 * n_chunks, HEAD), F32),
               big, big]
    res = pl.pallas_call(
        functools.partial(_dn_kernel, seq=seq, rope=rope, has_s0=has_s0, want_state=want_state),
        grid=(b, DN_HEADS),
        in_specs=in_specs,
        out_specs=out_specs,
        out_shape=out_shape,
        scratch_shapes=scratch,
        compiler_params=_params(("parallel", "parallel")),
        name="deltanet",
    )(*args)
    return res if want_state else (res[0], None)


def _out_kernel(oa_ref, ob_ref, x_ref, mod_ref, wo1_ref, wo2_ref, gpm_ref, gpf_ref, wq_ref, keys_ref,
                x1_ref, h2_ref, st_ref):
    mod = mod_ref[0]
    gate1 = mod[:, 2 * D:3 * D]
    shift2 = mod[:, 3 * D:4 * D]
    scale2 = mod[:, 4 * D:5 * D]
    y = _dot(oa_ref[...], wo1_ref[...]) + _dot(ob_ref[...], wo2_ref[...])
    x1 = x_ref[...] + gate1 * _rms(y, gpm_ref[...])
    x1_ref[...] = x1
    h2 = (_rms(x1, gpf_ref[...]) * (1.0 + scale2) + shift2).astype(BF16)
    h2_ref[...] = pltpu.bitcast(h2, jnp.uint32)
    q = _dot(h2, wq_ref[...]).astype(BF16)
    for hp in range(2 * PEER_HEADS):
        st_ref[hp] = lax.dot_general(keys_ref[hp], q[:, hp * HEAD:(hp + 1) * HEAD], _NT,
                                     preferred_element_type=F32)


def _out_call(oa, ob, x2, mod3, mod_row, wo1, wo2, gpm, gpf, wq, keys):
    t = x2.shape[0]
    tm = TM_IN
    full = lambda a: pl.BlockSpec(a.shape, lambda i: (0,) * a.ndim)
    return pl.pallas_call(
        _out_kernel,
        grid=(t // tm,),
        in_specs=[pl.BlockSpec((tm, 512), lambda i: (i, 0)),
                  pl.BlockSpec((tm, 512), lambda i: (i, 0)),
                  pl.BlockSpec((tm, D), lambda i: (i, 0)),
                  pl.BlockSpec((1, 1, 6 * D), lambda i: (mod_row(i, tm), 0, 0)),
                  full(wo1), full(wo2), full(gpm), full(gpf), full(wq), full(keys)],
        out_specs=[pl.BlockSpec((tm, D), lambda i: (i, 0)),
                   pl.BlockSpec((tm // 2, D), lambda i: (i, 0)),
                   pl.BlockSpec((2 * PEER_HEADS, NKEYS, tm), lambda i: (0, 0, i))],
        out_shape=[jax.ShapeDtypeStruct((t, D), F32),
                   jax.ShapeDtypeStruct((t // 2, D), jnp.uint32),
                   jax.ShapeDtypeStruct((2 * PEER_HEADS, NKEYS, t), F32)],
        compiler_params=_params(("parallel",)),
        name="out_proj",
    )(oa, ob, x2, mod3, wo1, wo2, gpm, gpf, wq, keys)


def _top16(x):
    vals = []
    rank = jnp.full(x.shape, float(TOPK), F32)
    for r in range(TOPK):
        m = jnp.max(x, axis=0, keepdims=True)
        vals.append(m)
        hit = x == m
        rank = jnp.where(hit, float(r), rank)
        if r + 1 < TOPK:
            x = jnp.where(hit, NEG_INF, x)
    return vals, rank


def _topk_kernel(s_ref, l1_ref, e1_ref, r2_ref, e2_ref):
    tl = s_ref.shape[-1]
    row16 = lax.broadcasted_iota(jnp.int32, (TOPK, tl), 0)
    row8 = lax.broadcasted_iota(jnp.int32, (8, tl), 0)

    def head(h, carry):
        s1 = s_ref[2 * h]
        s2 = s_ref[2 * h + 1]
        a, _ = _top16(s1)
        b, rank2 = _top16(s2)
        amat = jnp.concatenate(a, axis=0)
        bmat = jnp.concatenate(b, axis=0)
        a8, b8 = amat[:8], bmat[:8]
        cands = [a[0] + bmat,
                 jnp.where(row16 >= 2, amat + b[0], NEG_INF),
                 a[1] + b8,
                 jnp.where(row8 >= 2, a8 + b[1], NEG_INF),
                 jnp.where((row8 >= 2) & (row8 <= 4), a[2] + b8, NEG_INF),
                 jnp.where((row8 >= 2) & (row8 <= 3), a[3] + b8, NEG_INF),
                 jnp.where(row8 == 2, a[4] + b8, NEG_INF)]
        work = cands
        thr = None
        for r in range(TOPK):
            m16 = jnp.max(jnp.maximum(work[0], work[1]), axis=0, keepdims=True)
            m8 = jnp.max(functools.reduce(jnp.maximum, work[2:]), axis=0, keepdims=True)
            thr = jnp.maximum(m16, m8)
            if r + 1 < TOPK:
                work = [jnp.where(w == thr, NEG_INF, w) for w in work]
        m0 = a[0] + b[0]
        z = functools.reduce(
            jnp.add, [jnp.sum(jnp.where(cd >= thr, jnp.exp(cd - m0), 0.0), axis=0, keepdims=True) for cd in cands])
        cnt = jnp.zeros_like(s1)
        for j in range(TOPK):
            cnt = cnt + jnp.where(s1 + b[j] >= thr, 1.0, 0.0)
        l1_ref[h] = cnt
        r2_ref[h] = pltpu.bitcast(rank2.astype(BF16), jnp.uint32)
        e1_ref[h] = jnp.exp(s1 - a[0]) / z
        e2_ref[h] = pltpu.bitcast(jnp.exp(s2 - b[0]).astype(BF16), jnp.uint32)
        return carry

    lax.fori_loop(0, PEER_HEADS, head, 0)


def _topk_call(st, tl=TL_TOPK):
    t = st.shape[-1]
    spec = pl.BlockSpec((PEER_HEADS, NKEYS, tl), lambda i: (0, 0, i))
    spec_pk = pl.BlockSpec((PEER_HEADS, NKEYS // 2, tl), lambda i: (0, 0, i))
    f32 = jax.ShapeDtypeStruct((PEER_HEADS, NKEYS, t), F32)
    packed = jax.ShapeDtypeStruct((PEER_HEADS, NKEYS // 2, t), jnp.uint32)
    return pl.pallas_call(
        _topk_kernel,
        grid=(t // tl,),
        in_specs=[pl.BlockSpec((2 * PEER_HEADS, NKEYS, tl), lambda i: (0, 0, i))],
        out_specs=[spec, spec, spec_pk, spec_pk],
        out_shape=[f32, f32, packed, packed],
        compiler_params=_params(("parallel",)),
        name="peer_select",
    )(st)


def _peer_kernel(h2_ref, u_ref, vt_ref, l1_ref, e1_ref, r2_ref, e2_ref, acc_ref, ct_ref, *, tb):
    j = pl.program_id(1)
    tm = 2 * h2_ref.shape[0]
    ec = 2 * u_ref.shape[0]
    pk = 16

    @pl.when(j == 0)
    def _():
        acc_ref[...] = jnp.zeros_like(acc_ref)

    u = pltpu.bitcast(u_ref[...], BF16)
    vt = pltpu.bitcast(vt_ref[...], BF16)
    for lb in range(tm // tb):
        tcols = slice(lb * tb, (lb + 1) * tb)
        h2 = pltpu.bitcast(h2_ref[lb * tb // 2:(lb + 1) * tb // 2, :], BF16)
        at = lax.dot_general(u, h2, _NT, preferred_element_type=F32)
        for i1 in range(ec // NKEYS):
            for lh in range(tb // HEAD):
                cols = slice(lb * tb + lh * HEAD, lb * tb + (lh + 1) * HEAD)
                bcast = lambda ref, h: jnp.broadcast_to(ref[h, i1:i1 + 1, cols], (pk, HEAD)).astype(BF16)
                l1 = [bcast(l1_ref, h) for h in range(PEER_HEADS)]
                e1 = [bcast(e1_ref, h) for h in range(PEER_HEADS)]
                for pv in range(NKEYS // pk):
                    keys2 = slice(pv * pk // 2, (pv + 1) * pk // 2)
                    g = None
                    for h in range(PEER_HEADS):
                        r2 = pltpu.bitcast(r2_ref[h, keys2, cols], BF16)
                        e2 = pltpu.bitcast(e2_ref[h, keys2, cols], BF16)
                        term = jnp.where(r2 < l1[h], e2, 0.0) * e1[h]
                        g = term if g is None else g + term
                    rows = slice(i1 * NKEYS + pv * pk, i1 * NKEYS + (pv + 1) * pk)
                    act = _gelu(at[rows, lh * HEAD:(lh + 1) * HEAD].astype(BF16))
                    ct_ref[rows, cols] = g * act
        acc_ref[:, tcols] += _dot(vt, ct_ref[:, tcols])


def _peer_call(h2_pk, u_pk, vt_pk, l1, e1, r2, e2, tm=TM_PEER, ec=EC_PEER, tb=TB_PEER):
    t = 2 * h2_pk.shape[0]
    n_exp = 2 * u_pk.shape[0]
    i1_blk = ec // NKEYS
    return pl.pallas_call(
        functools.partial(_peer_kernel, tb=tb),
        grid=(t // tm, n_exp // ec),
        in_specs=[pl.BlockSpec((tm // 2, D), lambda i, j: (i, 0)),
                  pl.BlockSpec((ec // 2, D), lambda i, j: (j, 0)),
                  pl.BlockSpec((D // 2, ec), lambda i, j: (0, j)),
                  pl.BlockSpec((PEER_HEADS, i1_blk, tm), lambda i, j: (0, j, i)),
                  pl.BlockSpec((PEER_HEADS, i1_blk, tm), lambda i, j: (0, j, i)),
                  pl.BlockSpec((PEER_HEADS, NKEYS // 2, tm), lambda i, j: (0, 0, i)),
                  pl.BlockSpec((PEER_HEADS, NKEYS // 2, tm), lambda i, j: (0, 0, i))],
        out_specs=pl.BlockSpec((D, tm), lambda i, j: (0, i)),
        out_shape=jax.ShapeDtypeStruct((D, t), F32),
        scratch_shapes=[pltpu.VMEM((ec, tm), BF16)],
        compiler_params=_params(("parallel", "arbitrary")),
        name="peer",
    )(h2_pk, u_pk, vt_pk, l1, e1, r2, e2)


def _fin_kernel(ot_ref, x1_ref, mod_ref, gpost_ref, y_ref):
    gate2 = mod_ref[0][:, 5 * D:6 * D]
    y_ref[...] = x1_ref[...] + gate2 * _rms(ot_ref[...].T, gpost_ref[...])


def _fin_call(ot, x1, mod3, mod_row, gpost):
    t = x1.shape[0]
    tm = TM_IN
    return pl.pallas_call(
        _fin_kernel,
        grid=(t // tm,),
        in_specs=[pl.BlockSpec((D, tm), lambda i: (0, i)),
                  pl.BlockSpec((tm, D), lambda i: (i, 0)),
                  pl.BlockSpec((1, 1, 6 * D), lambda i: (mod_row(i, tm), 0, 0)),
                  pl.BlockSpec((1, D), lambda i: (0, 0))],
        out_specs=pl.BlockSpec((tm, D), lambda i: (i, 0)),
        out_shape=jax.ShapeDtypeStruct((t, D), F32),
        compiler_params=_params(("parallel",)),
        name="peer_residual",
    )(ot, x1, mod3, gpost)


def _rope_tables(seq):
    pos = jnp.arange(seq)
    quarter = HEAD // 4
    inv = ROPE_BASE ** (-jnp.arange(quarter, dtype=F32) / quarter)
    ang_r = (pos // GRID_W).astype(F32)[:, None] * inv
    ang_c = (pos % GRID_W).astype(F32)[:, None] * inv
    cos = jnp.concatenate([jnp.cos(ang_r)] * 2 + [jnp.cos(ang_c)] * 2, axis=-1)
    sin = jnp.concatenate([-jnp.sin(ang_r), jnp.sin(ang_r), -jnp.sin(ang_c), jnp.sin(ang_c)], axis=-1)
    return cos, sin


def _layer(x, mod3, mod_row, s0, rope, want_state, w):
    b, seq, _ = x.shape
    x2 = x.reshape(b * seq, D)
    oa, qkv, zs, gates = _in_call(x2, mod3, mod_row, w["gpre"], w["wgu"], w["wgv"], w["wqkv"], w["wz"], w["wab"],
                                  w["ws"], w["bs"], w["alog"], w["dtb"])
    cos, sin = _rope_tables(seq) if rope else (None, None)
    ob, st = _dn_call(qkv.reshape(b, seq, -1), gates.reshape(b, seq, HEAD), zs.reshape(b, seq, -1),
                      w["conv"], w["dnorm"], cos, sin, s0, want_state)
    x1, h2, sc = _out_call(oa, ob.reshape(b * seq, -1), x2, mod3, mod_row, w["wo1"], w["wo2"], w["gpm"], w["gpf"],
                           w["wq"], w["keys"])
    l1, e1, r2, e2 = _topk_call(sc)
    ot = _peer_call(h2, w["u"], w["vt"], l1, e1, r2, e2)
    y = _fin_call(ot, x1, mod3, mod_row, w["gpost"])
    return y.reshape(b, seq, D), st


def kernel(x_prompt, x_sample, state_delta, c, c_ctx, w_mod, b_mod, g_pre_mix, g_post_mix, g_pre_ffn, g_post_ffn,
           w_in, w_out, gm_ws, gm_bs, dn_conv, dn_a_log, dn_dt_bias, dn_norm, peer_wq, peer_keys, peer_u, peer_v):
    depth = w_mod.shape[0]
    dec_batch, dec_seq, _ = x_sample.shape
    xp, xs = x_prompt, x_sample
    states = []
    cc = jnp.zeros((16, D), F32).at[0].set(c_ctx).at[1:1 + dec_batch].set(c)
    for l in range(depth):
        mod3 = _mod_call(cc, w_mod[l], b_mod[l][None, :]).reshape(16, 1, 6 * D)
        wi = w_in[l]
        pad16 = lambda v: jnp.zeros((1, HEAD), F32).at[0, :v.size].set(v.reshape(-1))
        w = {
            "gpre": g_pre_mix[l][None, :], "gpm": g_post_mix[l][None, :], "gpf": g_pre_ffn[l][None, :],
            "gpost": g_post_ffn[l][None, :],
            "wgu": wi[:, 0:512].astype(BF16), "wgv": wi[:, 512:1024].astype(BF16),
            "wqkv": wi[:, 1024:2560].astype(BF16), "wz": wi[:, 2560:3072].astype(BF16),
            "wab": jnp.zeros((D, HEAD), F32).at[:, :16].set(wi[:, 3072:3088]).astype(BF16),
            "ws": gm_ws[l].astype(BF16),
            "bs": jnp.broadcast_to(gm_bs[l][:, :, None], (GM_HEADS, HEAD, HEAD)),
            "alog": pad16(dn_a_log[l]), "dtb": pad16(dn_dt_bias[l]),
            "conv": dn_conv[l], "dnorm": dn_norm[l][None, :],
            "wo1": w_out[l][:512].astype(BF16), "wo2": w_out[l][512:].astype(BF16),
            "wq": peer_wq[l].astype(BF16),
            "keys": peer_keys[l].reshape(2 * PEER_HEADS, NKEYS, HEAD).astype(BF16),
            "u": _pack_rows(peer_u[l].astype(BF16)), "vt": _pack_rows(peer_v[l].T.astype(BF16)),
        }
        xp, s_ctx = _layer(xp, mod3, lambda i, tm: 0, None, False, True, w)
        xs, _ = _layer(xs, mod3, lambda i, tm: 1 + (i * tm) // dec_seq, state_delta[:, l], True, False, w)
        states.append(s_ctx.astype(x_prompt.dtype))
    return xp, xs, jnp.stack(states, axis=1)
```

```python
import functools
import math

import jax
import jax.numpy as jnp
from jax import lax
from jax.experimental import pallas as pl
from jax.experimental.pallas import tpu as pltpu

F32 = jnp.float32
BF16 = jnp.bfloat16
HI = lax.Precision.HIGHEST
EPS = 1e-6
NEG_INF = float("-inf")

D = 1024
GM_HEADS = 4
DN_HEADS = 4
HEAD = 128
CHUNK = 64
GROUP = 4
GRID_W = 64
ROPE_BASE = 10000.0
PEER_HEADS = 8
NKEYS = 128
TOPK = 16

TM_IN = 512
TM_PEER = 2048
EC_PEER = 1024
TB_PEER = 512
TL_TOPK = 256
VMEM_LIMIT = 56 * 1024 * 1024

_NT = (((1,), (1,)), ((), ()))
_TN = (((0,), (0,)), ((), ()))


def _dot(a, b, precision=None):
    return jnp.dot(a, b, preferred_element_type=F32, precision=precision)


def _pack_kernel(x_ref, o_ref, *, transpose):
    x = x_ref[...]
    if transpose:
        x = x.T
    o_ref[...] = pltpu.bitcast(x.astype(BF16), jnp.uint32)


def _pack_call(x, transpose=False, rows=512):
    r, c = x.shape
    if transpose:
        out_spec, out_shape = pl.BlockSpec((c // 2, rows), lambda i: (0, i)), (c // 2, r)
    else:
        out_spec, out_shape = pl.BlockSpec((rows // 2, c), lambda i: (i, 0)), (r // 2, c)
    return pl.pallas_call(
        functools.partial(_pack_kernel, transpose=transpose),
        grid=(r // rows,),
        in_specs=[pl.BlockSpec((rows, c), lambda i: (i, 0))],
        out_specs=out_spec,
        out_shape=jax.ShapeDtypeStruct(out_shape, jnp.uint32),
        compiler_params=_params(("parallel",)),
        name="pack_t" if transpose else "pack",
    )(x)


def _split(x):
    hi = x.astype(BF16)
    return hi, (x - hi.astype(F32)).astype(BF16)


def _dot_split_rhs(m, x):
    hi, lo = _split(x)
    return _dot(m, hi) + _dot(m, lo)


def _dot_split_lhs(x, m):
    hi, lo = _split(x)
    return _dot(hi, m) + _dot(lo, m)


def _dot3(a, b):
    ah, al = _split(a)
    bh, bl = _split(b)
    return _dot(ah, bh) + (_dot(ah, bl) + _dot(al, bh))


_GELU_C1 = math.sqrt(2.0 / math.pi)
_GELU_C2 = _GELU_C1 * 0.044715


def _gelu(x):
    hx = 0.5 * x
    return hx + hx * jnp.tanh(x * (_GELU_C1 + _GELU_C2 * (x * x)))


def _silu(x):
    return x * jax.nn.sigmoid(x)


def _rms(x, gain):
    return x * lax.rsqrt(jnp.mean(x * x, axis=-1, keepdims=True) + EPS) * gain


def _params(sem):
    return pltpu.CompilerParams(dimension_semantics=sem, vmem_limit_bytes=VMEM_LIMIT)


def _mod_kernel(c_ref, w_ref, b_ref, o_ref):
    o_ref[...] = _dot(_silu(c_ref[...]), w_ref[...], HI) + b_ref[...]


def _mod_call(cc, w_mod, b_mod):
    n = w_mod.shape[1]
    bn = 1536
    return pl.pallas_call(
        _mod_kernel,
        grid=(n // bn,),
        in_specs=[pl.BlockSpec(cc.shape, lambda j: (0, 0)),
                  pl.BlockSpec((D, bn), lambda j: (0, j)),
                  pl.BlockSpec((1, bn), lambda j: (0, j))],
        out_specs=pl.BlockSpec((cc.shape[0], bn), lambda j: (0, j)),
        out_shape=jax.ShapeDtypeStruct((cc.shape[0], n), F32),
        compiler_params=_params(("arbitrary",)),
        name="mod",
    )(cc, w_mod, b_mod)


def _in_kernel(x_ref, mod_ref, gpre_ref, wgu_ref, wgv_ref, wqkv_ref, wz_ref, wab_ref, ws_ref, bs_ref,
               alog_ref, dtb_ref, oa_ref, qkv_ref, zs_ref, gates_ref):
    x = x_ref[...]
    mod = mod_ref[0]
    shift1 = mod[:, 0:D]
    scale1 = mod[:, D:2 * D]
    hb = (_rms(x, gpre_ref[...]) * (1.0 + scale1) + shift1).astype(BF16)
    gu = _gelu(_dot(hb, wgu_ref[...]))
    gv = _gelu(_dot(hb, wgv_ref[...]))
    qkv_ref[...] = _dot(hb, wqkv_ref[...])
    zs_ref[...] = _silu(_dot(hb, wz_ref[...]))
    pab = _dot(hb, wab_ref[...])
    t = pab + dtb_ref[...]
    softplus = jnp.maximum(t, 0.0) + jnp.log(1.0 + jnp.exp(-jnp.abs(t)))
    lane = lax.broadcasted_iota(jnp.int32, pab.shape, 1)
    gates_ref[...] = jnp.where(lane < 2 * DN_HEADS, -jnp.exp(alog_ref[...]) * softplus, jax.nn.sigmoid(pab))
    for hd in range(GM_HEADS):
        cols = slice(hd * HEAD, (hd + 1) * HEAD)
        v_h = gv[:, cols]
        vn = (v_h * lax.rsqrt(jnp.mean(v_h * v_h, axis=-1, keepdims=True) + EPS)).astype(BF16)
        for c in range(x.shape[0] // HEAD):
            rows = slice(c * HEAD, (c + 1) * HEAD)
            s = _dot(ws_ref[hd], vn[rows]) + bs_ref[hd]
            oa_ref[rows, cols] = (gu[rows, cols] * s).astype(BF16)


def _in_call(x2, mod3, mod_row, gpre, wgu, wgv, wqkv, wz, wab, ws, bs, alog, dtb):
    t = x2.shape[0]
    tm = TM_IN
    full = lambda a: pl.BlockSpec(a.shape, lambda i: (0,) * a.ndim)
    return pl.pallas_call(
        _in_kernel,
        grid=(t // tm,),
        in_specs=[pl.BlockSpec((tm, D), lambda i: (i, 0)),
                  pl.BlockSpec((1, 1, 6 * D), lambda i: (mod_row(i, tm), 0, 0)),
                  full(gpre), full(wgu), full(wgv), full(wqkv), full(wz), full(wab), full(ws), full(bs),
                  full(alog), full(dtb)],
        out_specs=[pl.BlockSpec((tm, 512), lambda i: (i, 0)),
                   pl.BlockSpec((tm, 1536), lambda i: (i, 0)),
                   pl.BlockSpec((tm, 512), lambda i: (i, 0)),
                   pl.BlockSpec((tm, HEAD), lambda i: (i, 0))],
        out_shape=[jax.ShapeDtypeStruct((t, 512), BF16),
                   jax.ShapeDtypeStruct((t, 1536), F32),
                   jax.ShapeDtypeStruct((t, 512), F32),
                   jax.ShapeDtypeStruct((t, HEAD), F32)],
        compiler_params=_params(("parallel",)),
        name="in_proj",
    )(x2, mod3, gpre, wgu, wgv, wqkv, wz, wab, ws, bs, alog, dtb)


def _dn_kernel(*refs, seq, rope, has_s0, want_state):
    it = iter(refs)
    q_ref, k_ref, v_ref, gates_ref, zs_ref = next(it), next(it), next(it), next(it), next(it)
    cwq_ref, cwk_ref, cwv_ref, nw_ref = next(it), next(it), next(it), next(it)
    cos_ref = next(it) if rope else None
    sin_ref = next(it) if rope else None
    s0_ref = next(it) if has_s0 else None
    ob_ref = next(it)
    st_ref = next(it) if want_state else None
    q_s, k_s, v_s, gf_s, gb_s, bf_s, bb_s, mqf_s, mqb_s, nf_s, nb_s, glf_s, glb_s, of_s, obk_s = it
    n_chunks = seq // CHUNK
    hd = pl.program_id(1)

    row = lax.broadcasted_iota(jnp.int32, (seq, HEAD), 0)
    lane = lax.broadcasted_iota(jnp.int32, (seq, HEAD), 1)

    def conv_silu(ref, cw_ref):
        x = ref[0]
        w = cw_ref[...]
        xm = jnp.where(row == 0, 0.0, pltpu.roll(x, 1, axis=0))
        xp = jnp.where(row == seq - 1, 0.0, pltpu.roll(x, seq - 1, axis=0))
        return _silu(xm * w[0:1] + x * w[1:2] + xp * w[2:3])

    def l2n(x):
        return x * lax.rsqrt(jnp.sum(x * x, axis=-1, keepdims=True) + EPS)

    def rot(x):
        if not rope:
            return x
        partner = jnp.where((lane & 32) == 0, pltpu.roll(x, 96, axis=1), pltpu.roll(x, 32, axis=1))
        return x * cos_ref[...] + partner * sin_ref[...]

    q_s[...] = rot(l2n(conv_silu(q_ref, cwq_ref))) * (HEAD ** -0.5)
    k_s[...] = rot(l2n(conv_silu(k_ref, cwk_ref)))
    v_s[...] = conv_silu(v_ref, cwv_ref)

    gates = gates_ref[0]
    sel_r = lax.broadcasted_iota(jnp.int32, (HEAD, HEAD), 0)
    for dst, col in ((gf_s, hd), (gb_s, DN_HEADS + hd), (bf_s, 2 * DN_HEADS + hd), (bb_s, 3 * DN_HEADS + hd)):
        dst[...] = _dot_split_lhs(gates, (sel_r == col).astype(BF16))

    ri = lax.broadcasted_iota(jnp.int32, (CHUNK, HEAD), 0)
    li = lax.broadcasted_iota(jnp.int32, (CHUNK, HEAD), 1)
    lj = li & (CHUNK - 1)
    is_f = li < CHUNK
    is_b = li >= CHUNK
    incl = (is_f & (ri >= lj)) | (is_b & (ri <= lj))
    strict = (is_f & (ri > lj)) | (is_b & (ri < lj))
    diag = lj == ri
    ti = lax.broadcasted_iota(jnp.int32, (CHUNK, CHUNK), 0)
    tj = lax.broadcasted_iota(jnp.int32, (CHUNK, CHUNK), 1)
    lower = (tj <= ti).astype(BF16)
    upper = (tj >= ti).astype(BF16)
    ones = jnp.ones((CHUNK, CHUNK), BF16)
    ei = lax.broadcasted_iota(jnp.int32, (2 * CHUNK, 2 * CHUNK), 0)
    ej = lax.broadcasted_iota(jnp.int32, (2 * CHUNK, 2 * CHUNK), 1)
    eye = (ei == ej).astype(F32)

    mq_rows = HEAD + CHUNK

    def finish(c, rows, qc, kc, gcd, gxc, attn_d, wu_d, mq_s, n_s, gl_s, o_s):
        tot = jnp.sum(gxc, axis=0, keepdims=True)
        kd = (kc * jnp.exp(tot - gcd)).astype(BF16)
        wub = wu_d.astype(BF16)
        kw = lax.dot_general(kd, wub, _TN, preferred_element_type=F32)
        aw = _dot(attn_d.astype(BF16), wub)
        mq = jnp.concatenate([-kw[:, :HEAD], qc * jnp.exp(gcd) - aw[:, :HEAD]], axis=0)
        mq_s[pl.ds(pl.multiple_of(c * mq_rows, CHUNK), mq_rows), :] = mq.astype(BF16)
        n_s[pl.ds(pl.multiple_of(c * HEAD, HEAD), HEAD), :] = kw[:, HEAD:]
        o_s[rows, :] = aw[:, HEAD:]
        gl_s[pl.ds(pl.multiple_of(c * 8, 8), 8), :] = jnp.broadcast_to(jnp.exp(tot), (8, HEAD))

    def prep(gi, carry):
        cs = [gi * GROUP + g for g in range(GROUP)]
        rws = [pl.ds(pl.multiple_of(c * CHUNK, CHUNK), CHUNK) for c in cs]
        st = [dict(qc=q_s[r, :], kc=k_s[r, :], vc=v_s[r, :], gfc=gf_s[r, :], gbc=gb_s[r, :],
                   bfc=bf_s[r, :], bbc=bb_s[r, :]) for r in rws]
        for s in st:
            s["gc_f"] = _dot_split_rhs(lower, s["gfc"])
            s["gc_b"] = _dot_split_rhs(upper, s["gbc"])
            s["gc"] = jnp.where(is_f, s["gc_f"], s["gc_b"])
            kb = s["kc"].astype(BF16)
            kk2 = jnp.concatenate([kb, kb], axis=0)
            s["kk"] = lax.dot_general(kb, kk2, _NT, preferred_element_type=F32)
            s["qk"] = lax.dot_general(s["qc"].astype(BF16), kk2, _NT, preferred_element_type=F32)
        for s in st:
            s["gc_row"] = _dot_split_rhs(ones, jnp.where(diag, s["gc"], 0.0))
        for s in st:
            dec = jnp.exp(jnp.where(incl, s["gc"] - s["gc_row"], NEG_INF))
            beta = jnp.where(is_f, s["bfc"], s["bbc"])
            a = jnp.where(strict, beta * s["kk"] * dec, 0.0)
            s["x"] = -jnp.concatenate([jnp.where(is_f, a, 0.0), jnp.where(is_f, 0.0, a)], axis=0)
            s["p"] = eye + s["x"]
            s["attn"] = s["qk"] * dec
        for _ in range(5):
            for s in st:
                s["x"] = _dot3(s["x"], s["x"])
            for s in st:
                s["p"] = s["p"] + _dot3(s["p"], s["x"])
        for s in st:
            kc, vc, bfc, bbc = s["kc"], s["vc"], s["bfc"], s["bbc"]
            rhs = jnp.concatenate(
                [jnp.concatenate([kc * bfc * jnp.exp(s["gc_f"]), vc * bfc], axis=1),
                 jnp.concatenate([kc * bbc * jnp.exp(s["gc_b"]), vc * bbc], axis=1)], axis=0)
            s["wu"] = _dot(s["p"].astype(BF16), rhs.astype(BF16))
        for s, c, r in zip(st, cs, rws):
            finish(c, r, s["qc"], s["kc"], s["gc_f"], s["gfc"], s["attn"][:, :CHUNK], s["wu"][:CHUNK],
                   mqf_s, nf_s, glf_s, of_s)
            finish(c, r, s["qc"], s["kc"], s["gc_b"], s["gbc"], pltpu.roll(s["attn"], CHUNK, axis=1)[:, :CHUNK],
                   s["wu"][CHUNK:], mqb_s, nb_s, glb_s, obk_s)
        return carry

    lax.fori_loop(0, n_chunks // GROUP, prep, 0)

    def one_dir(c, s, mq_s, n_s, gl_s, o_s):
        rows = pl.ds(pl.multiple_of(c * CHUNK, CHUNK), CHUNK)
        r = _dot(mq_s[pl.ds(pl.multiple_of(c * mq_rows, CHUNK), mq_rows), :], s.astype(BF16))
        o_s[rows, :] += r[HEAD:]
        gl = gl_s[pl.ds(pl.multiple_of(c * 8, 8), 1), :]
        return s * gl + (r[:HEAD] + n_s[pl.ds(pl.multiple_of(c * HEAD, HEAD), HEAD), :])

    def scan(i, carry):
        s_f, s_b = carry
        s_f = one_dir(i, s_f, mqf_s, nf_s, glf_s, of_s)
        s_b = one_dir(n_chunks - 1 - i, s_b, mqb_s, nb_s, glb_s, obk_s)
        return s_f, s_b

    if has_s0:
        init = (s0_ref[0, 0, 0], s0_ref[0, 1, 0])
    else:
        init = (jnp.zeros((HEAD, HEAD), F32), jnp.zeros((HEAD, HEAD), F32))
    s_f, s_b = lax.fori_loop(0, n_chunks, scan, init)
    if want_state:
        st_ref[0, 0, 0] = s_f
        st_ref[0, 1, 0] = s_b

    o = of_s[...] + obk_s[...]
    ob_ref[0] = (_rms(o, nw_ref[...]) * zs_ref[0]).astype(BF16)


def _dn_call(qkv3, gates3, zs3, conv_w, norm_w, cos, sin, s0, want_state):
    b, seq, _ = qkv3.shape
    rope = cos is not None
    has_s0 = s0 is not None
    n_chunks = seq // CHUNK
    col = lambda off: pl.BlockSpec((1, seq, HEAD), lambda bi, h: (bi, 0, off + h))
    cw = lambda off: pl.BlockSpec((3, HEAD), lambda bi, h: (0, off + h))
    in_specs = [col(0), col(DN_HEADS), col(2 * DN_HEADS),
                pl.BlockSpec((1, seq, HEAD), lambda bi, h: (bi, 0, 0)),
                pl.BlockSpec((1, seq, HEAD), lambda bi, h: (bi, 0, h)),
                cw(0), cw(DN_HEADS), cw(2 * DN_HEADS),
                pl.BlockSpec((1, HEAD), lambda bi, h: (0, 0))]
    args = [qkv3, qkv3, qkv3, gates3, zs3, conv_w, conv_w, conv_w, norm_w]
    if rope:
        in_specs += [pl.BlockSpec((seq, HEAD), lambda bi, h: (0, 0))] * 2
        args += [cos, sin]
    st_spec = pl.BlockSpec((1, 2, 1, HEAD, HEAD), lambda bi, h: (bi, 0, h, 0, 0))
    if has_s0:
        in_specs.append(st_spec)
        args.append(s0)
    out_specs = [pl.BlockSpec((1, seq, HEAD), lambda bi, h: (bi, 0, h))]
    out_shape = [jax.ShapeDtypeStruct((b, seq, DN_HEADS * HEAD), BF16)]
    if want_state:
        out_specs.append(st_spec)
        out_shape.append(jax.ShapeDtypeStruct((b, 2, DN_HEADS, HEAD, HEAD), F32))
    big = pltpu.VMEM((seq, HEAD), F32)
    scratch = [big, big, big, big, big, big, big,
               pltpu.VMEM((3 * seq, HEAD), BF16), pltpu.VMEM((3 * seq, HEAD), BF16),
               pltpu.VMEM((2 * seq, HEAD), F32), pltpu.VMEM((2 * seq, HEAD), F32),
               pltpu.VMEM((8 * n_chunks, HEAD), F32), pltpu.VMEM((8 * n_chunks, HEAD), F32),
               big, big]
    res = pl.pallas_call(
        functools.partial(_dn_kernel, seq=seq, rope=rope, has_s0=has_s0, want_state=want_state),
        grid=(b, DN_HEADS),
        in_specs=in_specs,
        out_specs=out_specs,
        out_shape=out_shape,
        scratch_shapes=scratch,
        compiler_params=_params(("parallel", "parallel")),
        name="deltanet",
    )(*args)
    return res if want_state else (res[0], None)


def _out_kernel(oa_ref, ob_ref, x_ref, mod_ref, wo1_ref, wo2_ref, gpm_ref, gpf_ref, wq_ref, keys_ref,
                x1_ref, h2_ref, st_ref):
    mod = mod_ref[0]
    gate1 = mod[:, 2 * D:3 * D]
    shift2 = mod[:, 3 * D:4 * D]
    scale2 = mod[:, 4 * D:5 * D]
    y = _dot(oa_ref[...], wo1_ref[...]) + _dot(ob_ref[...], wo2_ref[...])
    x1 = x_ref[...] + gate1 * _rms(y, gpm_ref[...])
    x1_ref[...] = x1
    h2 = (_rms(x1, gpf_ref[...]) * (1.0 + scale2) + shift2).astype(BF16)
    h2_ref[...] = pltpu.bitcast(h2, jnp.uint32)
    q = _dot(h2, wq_ref[...]).astype(BF16)
    for hp in range(2 * PEER_HEADS):
        st_ref[hp] = lax.dot_general(keys_ref[hp], q[:, hp * HEAD:(hp + 1) * HEAD], _NT,
                                     preferred_element_type=F32)


def _out_call(oa, ob, x2, mod3, mod_row, wo1, wo2, gpm, gpf, wq, keys):
    t = x2.shape[0]
    tm = TM_IN
    full = lambda a: pl.BlockSpec(a.shape, lambda i: (0,) * a.ndim)
    return pl.pallas_call(
        _out_kernel,
        grid=(t // tm,),
        in_specs=[pl.BlockSpec((tm, 512), lambda i: (i, 0)),
                  pl.BlockSpec((tm, 512), lambda i: (i, 0)),
                  pl.BlockSpec((tm, D), lambda i: (i, 0)),
                  pl.BlockSpec((1, 1, 6 * D), lambda i: (mod_row(i, tm), 0, 0)),
                  full(wo1), full(wo2), full(gpm), full(gpf), full(wq), full(keys)],
        out_specs=[pl.BlockSpec((tm, D), lambda i: (i, 0)),
                   pl.BlockSpec((tm // 2, D), lambda i: (i, 0)),
                   pl.BlockSpec((2 * PEER_HEADS, NKEYS, tm), lambda i: (0, 0, i))],
        out_shape=[jax.ShapeDtypeStruct((t, D), F32),
                   jax.ShapeDtypeStruct((t // 2, D), jnp.uint32),
                   jax.ShapeDtypeStruct((2 * PEER_HEADS, NKEYS, t), F32)],
        compiler_params=_params(("parallel",)),
        name="out_proj",
    )(oa, ob, x2, mod3, wo1, wo2, gpm, gpf, wq, keys)


def _top16(x):
    vals = []
    rank = jnp.full(x.shape, float(TOPK), F32)
    for r in range(TOPK):
        m = jnp.max(x, axis=0, keepdims=True)
        vals.append(m)
        hit = x == m
        rank = jnp.where(hit, float(r), rank)
        if r + 1 < TOPK:
            x = jnp.where(hit, NEG_INF, x)
    return vals, rank


def _topk_kernel(s_ref, l1_ref, e1_ref, r2_ref, e2_ref):
    tl = s_ref.shape[-1]
    row16 = lax.broadcasted_iota(jnp.int32, (TOPK, tl), 0)
    row8 = lax.broadcasted_iota(jnp.int32, (8, tl), 0)

    def head(h, carry):
        s1 = s_ref[2 * h]
        s2 = s_ref[2 * h + 1]
        a, _ = _top16(s1)
        b, rank2 = _top16(s2)
        amat = jnp.concatenate(a, axis=0)
        bmat = jnp.concatenate(b, axis=0)
        a8, b8 = amat[:8], bmat[:8]
        cands = [a[0] + bmat,
                 jnp.where(row16 >= 2, amat + b[0], NEG_INF),
                 a[1] + b8,
                 jnp.where(row8 >= 2, a8 + b[1], NEG_INF),
                 jnp.where((row8 >= 2) & (row8 <= 4), a[2] + b8, NEG_INF),
                 jnp.where((row8 >= 2) & (row8 <= 3), a[3] + b8, NEG_INF),
                 jnp.where(row8 == 2, a[4] + b8, NEG_INF)]
        work = cands
        thr = None
        for r in range(TOPK):
            m16 = jnp.max(jnp.maximum(work[0], work[1]), axis=0, keepdims=True)
            m8 = jnp.max(functools.reduce(jnp.maximum, work[2:]), axis=0, keepdims=True)
            thr = jnp.maximum(m16, m8)
            if r + 1 < TOPK:
                work = [jnp.where(w == thr, NEG_INF, w) for w in work]
        m0 = a[0] + b[0]
        z = functools.reduce(
            jnp.add, [jnp.sum(jnp.where(cd >= thr, jnp.exp(cd - m0), 0.0), axis=0, keepdims=True) for cd in cands])
        cnt = jnp.zeros_like(s1)
        for j in range(TOPK):
            cnt = cnt + jnp.where(s1 + b[j] >= thr, 1.0, 0.0)
        l1_ref[h] = cnt
        r2_ref[h] = pltpu.bitcast(rank2.astype(BF16), jnp.uint32)
        e1_ref[h] = jnp.exp(s1 - a[0]) / z
        e2_ref[h] = pltpu.bitcast(jnp.exp(s2 - b[0]).astype(BF16), jnp.uint32)
        return carry

    lax.fori_loop(0, PEER_HEADS, head, 0)


def _topk_call(st, tl=TL_TOPK):
    t = st.shape[-1]
    spec = pl.BlockSpec((PEER_HEADS, NKEYS, tl), lambda i: (0, 0, i))
    spec_pk = pl.BlockSpec((PEER_HEADS, NKEYS // 2, tl), lambda i: (0, 0, i))
    f32 = jax.ShapeDtypeStruct((PEER_HEADS, NKEYS, t), F32)
    packed = jax.ShapeDtypeStruct((PEER_HEADS, NKEYS // 2, t), jnp.uint32)
    return pl.pallas_call(
        _topk_kernel,
        grid=(t // tl,),
        in_specs=[pl.BlockSpec((2 * PEER_HEADS, NKEYS, tl), lambda i: (0, 0, i))],
        out_specs=[spec, spec, spec_pk, spec_pk],
        out_shape=[f32, f32, packed, packed],
        compiler_params=_params(("parallel",)),
        name="peer_select",
    )(st)


def _peer_kernel(h2_ref, u_ref, vt_ref, l1_ref, e1_ref, r2_ref, e2_ref, acc_ref, *ct_refs, tb):
    j = pl.program_id(1)
    tm = 2 * h2_ref.shape[0]
    ec = 2 * u_ref.shape[0]
    pk = 16

    @pl.when(j == 0)
    def _():
        acc_ref[...] = jnp.zeros_like(acc_ref)

    u = pltpu.bitcast(u_ref[...], BF16)
    vt = pltpu.bitcast(vt_ref[...], BF16)
    heads = range(PEER_HEADS)
    pair = 2
    n_pv = NKEYS // pk

    def scores(lb):
        h2 = pltpu.bitcast(h2_ref[lb * tb // 2:(lb + 1) * tb // 2, :], BF16)
        return lax.dot_general(u, h2, _NT, preferred_element_type=F32)

    def gate(lb, at, lhs):
        for lh in lhs:
            cols = slice(lb * tb + lh * HEAD, lb * tb + (lh + 1) * HEAD)
            for i0 in range(0, ec // NKEYS, pair):
                bcast = lambda ref, h, i1: jnp.broadcast_to(ref[h, i1:i1 + 1, cols], (pk, HEAD)).astype(BF16)
                g = [[None] * n_pv for _ in range(pair)]
                for h in heads:
                    l1 = [bcast(l1_ref, h, i0 + d) for d in range(pair)]
                    e1 = [bcast(e1_ref, h, i0 + d) for d in range(pair)]
                    for pv in range(n_pv):
                        keys2 = slice(pv * pk // 2, (pv + 1) * pk // 2)
                        r2 = pltpu.bitcast(r2_ref[h, keys2, cols], BF16)
                        e2 = pltpu.bitcast(e2_ref[h, keys2, cols], BF16)
                        for d in range(pair):
                            term = jnp.where(r2 < l1[d], e2, 0.0) * e1[d]
                            g[d][pv] = term if g[d][pv] is None else g[d][pv] + term
                for d in range(pair):
                    for pv in range(n_pv):
                        rows = slice((i0 + d) * NKEYS + pv * pk, (i0 + d) * NKEYS + (pv + 1) * pk)
                        a = at[rows, lh * HEAD:(lh + 1) * HEAD]
                        th = jnp.tanh((a * (_GELU_C1 + _GELU_C2 * (a * a))).astype(BF16))
                        hx = 0.5 * a.astype(BF16)
                        ct_refs[lb][rows, lh * HEAD:(lh + 1) * HEAD] = g[d][pv] * (hx + hx * th)

    n_lb = tm // tb
    n_lh = tb // HEAD
    at = scores(0)
    for lb in range(n_lb):
        gate(lb, at, range(0, n_lh // 2))
        at_next = scores(lb + 1) if lb + 1 < n_lb else None
        gate(lb, at, range(n_lh // 2, n_lh))
        tcols = slice(lb * tb, (lb + 1) * tb)
        acc_ref[:, tcols] += _dot(vt, ct_refs[lb][...])
        at = at_next


def _peer_call(h2_pk, u_pk, vt_pk, l1, e1, r2, e2, tm=TM_PEER, ec=EC_PEER, tb=TB_PEER):
    t = 2 * h2_pk.shape[0]
    n_exp = 2 * u_pk.shape[0]
    i1_blk = ec // NKEYS
    return pl.pallas_call(
        functools.partial(_peer_kernel, tb=tb),
        grid=(t // tm, n_exp // ec),
        in_specs=[pl.BlockSpec((tm // 2, D), lambda i, j: (i, 0), pipeline_mode=pl.Buffered(1)),
                  pl.BlockSpec((ec // 2, D), lambda i, j: (j, 0)),
                  pl.BlockSpec((D // 2, ec), lambda i, j: (0, j)),
                  pl.BlockSpec((PEER_HEADS, i1_blk, tm), lambda i, j: (0, j, i)),
                  pl.BlockSpec((PEER_HEADS, i1_blk, tm), lambda i, j: (0, j, i)),
                  pl.BlockSpec((PEER_HEADS, NKEYS // 2, tm), lambda i, j: (0, 0, i), pipeline_mode=pl.Buffered(1)),
                  pl.BlockSpec((PEER_HEADS, NKEYS // 2, tm), lambda i, j: (0, 0, i), pipeline_mode=pl.Buffered(1))],
        out_specs=pl.BlockSpec((D, tm), lambda i, j: (0, i)),
        out_shape=jax.ShapeDtypeStruct((D, t), F32),
        scratch_shapes=[pltpu.VMEM((ec, tb), BF16)] * (tm // tb),
        compiler_params=_params(("parallel", "arbitrary")),
        name="peer",
    )(h2_pk, u_pk, vt_pk, l1, e1, r2, e2)


def _fin_kernel(ot_ref, x1_ref, mod_ref, gpost_ref, y_ref):
    gate2 = mod_ref[0][:, 5 * D:6 * D]
    y_ref[...] = x1_ref[...] + gate2 * _rms(ot_ref[...].T, gpost_ref[...])


def _fin_call(ot, x1, mod3, mod_row, gpost):
    t = x1.shape[0]
    tm = TM_IN
    return pl.pallas_call(
        _fin_kernel,
        grid=(t // tm,),
        in_specs=[pl.BlockSpec((D, tm), lambda i: (0, i)),
                  pl.BlockSpec((tm, D), lambda i: (i, 0)),
                  pl.BlockSpec((1, 1, 6 * D), lambda i: (mod_row(i, tm), 0, 0)),
                  pl.BlockSpec((1, D), lambda i: (0, 0))],
        out_specs=pl.BlockSpec((tm, D), lambda i: (i, 0)),
        out_shape=jax.ShapeDtypeStruct((t, D), F32),
        compiler_params=_params(("parallel",)),
        name="peer_residual",
    )(ot, x1, mod3, gpost)


def _rope_tables(seq):
    pos = jnp.arange(seq)
    quarter = HEAD // 4
    inv = ROPE_BASE ** (-jnp.arange(quarter, dtype=F32) / quarter)
    ang_r = (pos // GRID_W).astype(F32)[:, None] * inv
    ang_c = (pos % GRID_W).astype(F32)[:, None] * inv
    cos = jnp.concatenate([jnp.cos(ang_r)] * 2 + [jnp.cos(ang_c)] * 2, axis=-1)
    sin = jnp.concatenate([-jnp.sin(ang_r), jnp.sin(ang_r), -jnp.sin(ang_c), jnp.sin(ang_c)], axis=-1)
    return cos, sin


def _layer(x, mod3, mod_row, s0, rope, want_state, w):
    b, seq, _ = x.shape
    x2 = x.reshape(b * seq, D)
    oa, qkv, zs, gates = _in_call(x2, mod3, mod_row, w["gpre"], w["wgu"], w["wgv"], w["wqkv"], w["wz"], w["wab"],
                                  w["ws"], w["bs"], w["alog"], w["dtb"])
    cos, sin = _rope_tables(seq) if rope else (None, None)
    ob, st = _dn_call(qkv.reshape(b, seq, -1), gates.reshape(b, seq, HEAD), zs.reshape(b, seq, -1),
                      w["conv"], w["dnorm"], cos, sin, s0, want_state)
    x1, h2, sc = _out_call(oa, ob.reshape(b * seq, -1), x2, mod3, mod_row, w["wo1"], w["wo2"], w["gpm"], w["gpf"],
                           w["wq"], w["keys"])
    l1, e1, r2, e2 = _topk_call(sc)
    ot = _peer_call(h2, w["u"], w["vt"], l1, e1, r2, e2)
    y = _fin_call(ot, x1, mod3, mod_row, w["gpost"])
    return y.reshape(b, seq, D), st


def kernel(x_prompt, x_sample, state_delta, c, c_ctx, w_mod, b_mod, g_pre_mix, g_post_mix, g_pre_ffn, g_post_ffn,
           w_in, w_out, gm_ws, gm_bs, dn_conv, dn_a_log, dn_dt_bias, dn_norm, peer_wq, peer_keys, peer_u, peer_v):
    depth = w_mod.shape[0]
    dec_batch, dec_seq, _ = x_sample.shape
    xp, xs = x_prompt, x_sample
    states = []
    cc = jnp.zeros((16, D), F32).at[0].set(c_ctx).at[1:1 + dec_batch].set(c)
    for l in range(depth):
        mod3 = _mod_call(cc, w_mod[l], b_mod[l][None, :]).reshape(16, 1, 6 * D)
        wi = w_in[l]
        pad16 = lambda v: jnp.zeros((1, HEAD), F32).at[0, :v.size].set(v.reshape(-1))
        w = {
            "gpre": g_pre_mix[l][None, :], "gpm": g_post_mix[l][None, :], "gpf": g_pre_ffn[l][None, :],
            "gpost": g_post_ffn[l][None, :],
            "wgu": wi[:, 0:512].astype(BF16), "wgv": wi[:, 512:1024].astype(BF16),
            "wqkv": wi[:, 1024:2560].astype(BF16), "wz": wi[:, 2560:3072].astype(BF16),
            "wab": jnp.zeros((D, HEAD), F32).at[:, :16].set(wi[:, 3072:3088]).astype(BF16),
            "ws": gm_ws[l].astype(BF16),
            "bs": jnp.broadcast_to(gm_bs[l][:, :, None], (GM_HEADS, HEAD, HEAD)),
            "alog": pad16(dn_a_log[l]), "dtb": pad16(dn_dt_bias[l]),
            "conv": dn_conv[l], "dnorm": dn_norm[l][None, :],
            "wo1": w_out[l][:512].astype(BF16), "wo2": w_out[l][512:].astype(BF16),
            "wq": peer_wq[l].astype(BF16),
            "keys": peer_keys[l].reshape(2 * PEER_HEADS, NKEYS, HEAD).astype(BF16),
            "u": _pack_call(peer_u[l]), "vt": _pack_call(peer_v[l], transpose=True),
        }
        xp, s_ctx = _layer(xp, mod3, lambda i, tm: 0, None, False, True, w)
        xs, _ = _layer(xs, mod3, lambda i, tm: 1 + (i * tm) // dec_seq, state_delta[:, l], True, False, w)
        states.append(s_ctx.astype(x_prompt.dtype))
    return xp, xs, jnp.stack(states, axis=1)
```

```python
import functools
import math

import jax
import jax.numpy as jnp
from jax import lax
from jax.experimental import pallas as pl
from jax.experimental.pallas import tpu as pltpu

F32 = jnp.float32
BF16 = jnp.bfloat16
HI = lax.Precision.HIGHEST
EPS = 1e-6
NEG_INF = float("-inf")

D = 1024
GM_HEADS = 4
DN_HEADS = 4
HEAD = 128
CHUNK = 64
GROUP = 4
GRID_W = 64
ROPE_BASE = 10000.0
PEER_HEADS = 8
NKEYS = 128
TOPK = 16

TM_IN = 512
TM_PEER = 1024
EC_PEER = 1024
TB_PEER = 512
VMEM_LIMIT = 56 * 1024 * 1024

_NT = (((1,), (1,)), ((), ()))
_TN = (((0,), (0,)), ((), ()))


def _dot(a, b, precision=None):
    return jnp.dot(a, b, preferred_element_type=F32, precision=precision)


def _pack_kernel(x_ref, o_ref, *, transpose):
    x = x_ref[...]
    if transpose:
        x = x.T
    o_ref[...] = pltpu.bitcast(x.astype(BF16), jnp.uint32)


def _pack_call(x, transpose=False, rows=512):
    r, c = x.shape
    if transpose:
        out_spec, out_shape = pl.BlockSpec((c // 2, rows), lambda i: (0, i)), (c // 2, r)
    else:
        out_spec, out_shape = pl.BlockSpec((rows // 2, c), lambda i: (i, 0)), (r // 2, c)
    return pl.pallas_call(
        functools.partial(_pack_kernel, transpose=transpose),
        grid=(r // rows,),
        in_specs=[pl.BlockSpec((rows, c), lambda i: (i, 0))],
        out_specs=out_spec,
        out_shape=jax.ShapeDtypeStruct(out_shape, jnp.uint32),
        compiler_params=_params(("parallel",)),
        name="pack_t" if transpose else "pack",
    )(x)


def _split(x):
    hi = x.astype(BF16)
    return hi, (x - hi.astype(F32)).astype(BF16)


def _dot_split_rhs(m, x):
    hi, lo = _split(x)
    return _dot(m, hi) + _dot(m, lo)


def _dot_split_lhs(x, m):
    hi, lo = _split(x)
    return _dot(hi, m) + _dot(lo, m)


def _dot3(a, b):
    ah, al = _split(a)
    bh, bl = _split(b)
    return _dot(ah, bh) + (_dot(ah, bl) + _dot(al, bh))


_GELU_C1 = math.sqrt(2.0 / math.pi)
_GELU_C2 = _GELU_C1 * 0.044715


def _gelu(x):
    hx = 0.5 * x
    return hx + hx * jnp.tanh(x * (_GELU_C1 + _GELU_C2 * (x * x)))


def _silu(x):
    return x * jax.nn.sigmoid(x)


def _rms(x, gain):
    return x * lax.rsqrt(jnp.mean(x * x, axis=-1, keepdims=True) + EPS) * gain


def _params(sem):
    return pltpu.CompilerParams(dimension_semantics=sem, vmem_limit_bytes=VMEM_LIMIT)


def _mod_kernel(c_ref, w_ref, b_ref, o_ref):
    o_ref[...] = _dot(_silu(c_ref[...]), w_ref[...], HI) + b_ref[...]


def _mod_call(cc, w_mod, b_mod):
    n = w_mod.shape[1]
    bn = 1536
    return pl.pallas_call(
        _mod_kernel,
        grid=(n // bn,),
        in_specs=[pl.BlockSpec(cc.shape, lambda j: (0, 0)),
                  pl.BlockSpec((D, bn), lambda j: (0, j)),
                  pl.BlockSpec((1, bn), lambda j: (0, j))],
        out_specs=pl.BlockSpec((cc.shape[0], bn), lambda j: (0, j)),
        out_shape=jax.ShapeDtypeStruct((cc.shape[0], n), F32),
        compiler_params=_params(("arbitrary",)),
        name="mod",
    )(cc, w_mod, b_mod)


def _in_kernel(x_ref, mod_ref, gpre_ref, wgu_ref, wgv_ref, wqkv_ref, wz_ref, wab_ref, ws_ref, bs_ref,
               alog_ref, dtb_ref, oa_ref, qkv_ref, zs_ref, gates_ref):
    x = x_ref[...]
    mod = mod_ref[0]
    shift1 = mod[:, 0:D]
    scale1 = mod[:, D:2 * D]
    hb = (_rms(x, gpre_ref[...]) * (1.0 + scale1) + shift1).astype(BF16)
    gu = _gelu(_dot(hb, wgu_ref[...]))
    gv = _gelu(_dot(hb, wgv_ref[...]))
    qkv_ref[...] = _dot(hb, wqkv_ref[...])
    zs_ref[...] = _silu(_dot(hb, wz_ref[...]))
    pab = _dot(hb, wab_ref[...])
    t = pab + dtb_ref[...]
    softplus = jnp.maximum(t, 0.0) + jnp.log(1.0 + jnp.exp(-jnp.abs(t)))
    lane = lax.broadcasted_iota(jnp.int32, pab.shape, 1)
    gates_ref[...] = jnp.where(lane < 2 * DN_HEADS, -jnp.exp(alog_ref[...]) * softplus, jax.nn.sigmoid(pab))
    for hd in range(GM_HEADS):
        cols = slice(hd * HEAD, (hd + 1) * HEAD)
        v_h = gv[:, cols]
        vn = (v_h * lax.rsqrt(jnp.mean(v_h * v_h, axis=-1, keepdims=True) + EPS)).astype(BF16)
        for c in range(x.shape[0] // HEAD):
            rows = slice(c * HEAD, (c + 1) * HEAD)
            s = _dot(ws_ref[hd], vn[rows]) + bs_ref[hd]
            oa_ref[rows, cols] = (gu[rows, cols] * s).astype(BF16)


def _in_call(x2, mod3, mod_row, gpre, wgu, wgv, wqkv, wz, wab, ws, bs, alog, dtb):
    t = x2.shape[0]
    tm = TM_IN
    full = lambda a: pl.BlockSpec(a.shape, lambda i: (0,) * a.ndim)
    return pl.pallas_call(
        _in_kernel,
        grid=(t // tm,),
        in_specs=[pl.BlockSpec((tm, D), lambda i: (i, 0)),
                  pl.BlockSpec((1, 1, 6 * D), lambda i: (mod_row(i, tm), 0, 0)),
                  full(gpre), full(wgu), full(wgv), full(wqkv), full(wz), full(wab), full(ws), full(bs),
                  full(alog), full(dtb)],
        out_specs=[pl.BlockSpec((tm, 512), lambda i: (i, 0)),
                   pl.BlockSpec((tm, 1536), lambda i: (i, 0)),
                   pl.BlockSpec((tm, 512), lambda i: (i, 0)),
                   pl.BlockSpec((tm, HEAD), lambda i: (i, 0))],
        out_shape=[jax.ShapeDtypeStruct((t, 512), BF16),
                   jax.ShapeDtypeStruct((t, 1536), F32),
                   jax.ShapeDtypeStruct((t, 512), F32),
                   jax.ShapeDtypeStruct((t, HEAD), F32)],
        compiler_params=_params(("parallel",)),
        name="in_proj",
    )(x2, mod3, gpre, wgu, wgv, wqkv, wz, wab, ws, bs, alog, dtb)


def _dn_kernel(*refs, seq, rope, has_s0, want_state):
    it = iter(refs)
    q_ref, k_ref, v_ref, gates_ref, zs_ref = next(it), next(it), next(it), next(it), next(it)
    cwq_ref, cwk_ref, cwv_ref, nw_ref = next(it), next(it), next(it), next(it)
    cos_ref = next(it) if rope else None
    sin_ref = next(it) if rope else None
    s0_ref = next(it) if has_s0 else None
    ob_ref = next(it)
    st_ref = next(it) if want_state else None
    q_s, k_s, v_s, gf_s, gb_s, bf_s, bb_s, mqf_s, mqb_s, nf_s, nb_s, glf_s, glb_s, of_s, obk_s = it
    n_chunks = seq // CHUNK
    hd = pl.program_id(1)

    row = lax.broadcasted_iota(jnp.int32, (seq, HEAD), 0)
    lane = lax.broadcasted_iota(jnp.int32, (seq, HEAD), 1)

    def conv_silu(ref, cw_ref):
        x = ref[0]
        w = cw_ref[...]
        xm = jnp.where(row == 0, 0.0, pltpu.roll(x, 1, axis=0))
        xp = jnp.where(row == seq - 1, 0.0, pltpu.roll(x, seq - 1, axis=0))
        return _silu(xm * w[0:1] + x * w[1:2] + xp * w[2:3])

    def l2n(x):
        return x * lax.rsqrt(jnp.sum(x * x, axis=-1, keepdims=True) + EPS)

    def rot(x):
        if not rope:
            return x
        partner = jnp.where((lane & 32) == 0, pltpu.roll(x, 96, axis=1), pltpu.roll(x, 32, axis=1))
        return x * cos_ref[...] + partner * sin_ref[...]

    q_s[...] = rot(l2n(conv_silu(q_ref, cwq_ref))) * (HEAD ** -0.5)
    k_s[...] = rot(l2n(conv_silu(k_ref, cwk_ref)))
    v_s[...] = conv_silu(v_ref, cwv_ref)

    gates = gates_ref[0]
    sel_r = lax.broadcasted_iota(jnp.int32, (HEAD, HEAD), 0)
    for dst, col in ((gf_s, hd), (gb_s, DN_HEADS + hd), (bf_s, 2 * DN_HEADS + hd), (bb_s, 3 * DN_HEADS + hd)):
        dst[...] = _dot_split_lhs(gates, (sel_r == col).astype(BF16))

    ri = lax.broadcasted_iota(jnp.int32, (CHUNK, HEAD), 0)
    li = lax.broadcasted_iota(jnp.int32, (CHUNK, HEAD), 1)
    lj = li & (CHUNK - 1)
    is_f = li < CHUNK
    is_b = li >= CHUNK
    incl = (is_f & (ri >= lj)) | (is_b & (ri <= lj))
    strict = (is_f & (ri > lj)) | (is_b & (ri < lj))
    diag = lj == ri
    ti = lax.broadcasted_iota(jnp.int32, (CHUNK, CHUNK), 0)
    tj = lax.broadcasted_iota(jnp.int32, (CHUNK, CHUNK), 1)
    lower = (tj <= ti).astype(BF16)
    upper = (tj >= ti).astype(BF16)
    ones = jnp.ones((CHUNK, CHUNK), BF16)
    ei = lax.broadcasted_iota(jnp.int32, (2 * CHUNK, 2 * CHUNK), 0)
    ej = lax.broadcasted_iota(jnp.int32, (2 * CHUNK, 2 * CHUNK), 1)
    eye = (ei == ej).astype(F32)

    mq_rows = HEAD + CHUNK

    def finish(c, rows, qc, kc, gcd, gxc, attn_d, wu_d, mq_s, n_s, gl_s, o_s):
        tot = jnp.sum(gxc, axis=0, keepdims=True)
        kd = (kc * jnp.exp(tot - gcd)).astype(BF16)
        wub = wu_d.astype(BF16)
        kw = lax.dot_general(kd, wub, _TN, preferred_element_type=F32)
        aw = _dot(attn_d.astype(BF16), wub)
        mq = jnp.concatenate([-kw[:, :HEAD], qc * jnp.exp(gcd) - aw[:, :HEAD]], axis=0)
        mq_s[pl.ds(pl.multiple_of(c * mq_rows, CHUNK), mq_rows), :] = mq.astype(BF16)
        n_s[pl.ds(pl.multiple_of(c * HEAD, HEAD), HEAD), :] = kw[:, HEAD:]
        o_s[rows, :] = aw[:, HEAD:]
        gl_s[pl.ds(pl.multiple_of(c * 8, 8), 8), :] = jnp.broadcast_to(jnp.exp(tot), (8, HEAD))

    def prep(gi, carry):
        cs = [gi * GROUP + g for g in range(GROUP)]
        rws = [pl.ds(pl.multiple_of(c * CHUNK, CHUNK), CHUNK) for c in cs]
        st = [dict(qc=q_s[r, :], kc=k_s[r, :], vc=v_s[r, :], gfc=gf_s[r, :], gbc=gb_s[r, :],
                   bfc=bf_s[r, :], bbc=bb_s[r, :]) for r in rws]
        for s in st:
            s["gc_f"] = _dot_split_rhs(lower, s["gfc"])
            s["gc_b"] = _dot_split_rhs(upper, s["gbc"])
            s["gc"] = jnp.where(is_f, s["gc_f"], s["gc_b"])
            kb = s["kc"].astype(BF16)
            kk2 = jnp.concatenate([kb, kb], axis=0)
            s["kk"] = lax.dot_general(kb, kk2, _NT, preferred_element_type=F32)
            s["qk"] = lax.dot_general(s["qc"].astype(BF16), kk2, _NT, preferred_element_type=F32)
        for s in st:
            s["gc_row"] = _dot_split_rhs(ones, jnp.where(diag, s["gc"], 0.0))
        for s in st:
            dec = jnp.exp(jnp.where(incl, s["gc"] - s["gc_row"], NEG_INF))
            beta = jnp.where(is_f, s["bfc"], s["bbc"])
            a = jnp.where(strict, beta * s["kk"] * dec, 0.0)
            s["x"] = -jnp.concatenate([jnp.where(is_f, a, 0.0), jnp.where(is_f, 0.0, a)], axis=0)
            s["p"] = eye + s["x"]
            s["attn"] = s["qk"] * dec
        for it in range(5):
            mm = _dot3 if it < 3 else (lambda a_, b_: _dot(a_.astype(BF16), b_.astype(BF16)))
            for s in st:
                s["x"] = mm(s["x"], s["x"])
            for s in st:
                s["p"] = s["p"] + mm(s["p"], s["x"])
        for s in st:
            kc, vc, bfc, bbc = s["kc"], s["vc"], s["bfc"], s["bbc"]
            rhs = jnp.concatenate(
                [jnp.concatenate([kc * bfc * jnp.exp(s["gc_f"]), vc * bfc], axis=1),
                 jnp.concatenate([kc * bbc * jnp.exp(s["gc_b"]), vc * bbc], axis=1)], axis=0)
            s["wu"] = _dot(s["p"].astype(BF16), rhs.astype(BF16))
        for s, c, r in zip(st, cs, rws):
            finish(c, r, s["qc"], s["kc"], s["gc_f"], s["gfc"], s["attn"][:, :CHUNK], s["wu"][:CHUNK],
                   mqf_s, nf_s, glf_s, of_s)
            finish(c, r, s["qc"], s["kc"], s["gc_b"], s["gbc"], pltpu.roll(s["attn"], CHUNK, axis=1)[:, :CHUNK],
                   s["wu"][CHUNK:], mqb_s, nb_s, glb_s, obk_s)
        return carry

    lax.fori_loop(0, n_chunks // GROUP, prep, 0)

    def one_dir(c, s, mq_s, n_s, gl_s, o_s):
        rows = pl.ds(pl.multiple_of(c * CHUNK, CHUNK), CHUNK)
        r = _dot(mq_s[pl.ds(pl.multiple_of(c * mq_rows, CHUNK), mq_rows), :], s.astype(BF16))
        o_s[rows, :] += r[HEAD:]
        gl = gl_s[pl.ds(pl.multiple_of(c * 8, 8), 1), :]
        return s * gl + (r[:HEAD] + n_s[pl.ds(pl.multiple_of(c * HEAD, HEAD), HEAD), :])

    def scan(i, carry):
        s_f, s_b = carry
        s_f = one_dir(i, s_f, mqf_s, nf_s, glf_s, of_s)
        s_b = one_dir(n_chunks - 1 - i, s_b, mqb_s, nb_s, glb_s, obk_s)
        return s_f, s_b

    if has_s0:
        init = (s0_ref[0, 0, 0], s0_ref[0, 1, 0])
    else:
        init = (jnp.zeros((HEAD, HEAD), F32), jnp.zeros((HEAD, HEAD), F32))
    s_f, s_b = lax.fori_loop(0, n_chunks, scan, init)
    if want_state:
        st_ref[0, 0, 0] = s_f
        st_ref[0, 1, 0] = s_b

    o = of_s[...] + obk_s[...]
    ob_ref[0] = (_rms(o, nw_ref[...]) * zs_ref[0]).astype(BF16)


def _dn_call(qkv3, gates3, zs3, conv_w, norm_w, cos, sin, s0, want_state):
    b, seq, _ = qkv3.shape
    rope = cos is not None
    has_s0 = s0 is not None
    n_chunks = seq // CHUNK
    col = lambda off: pl.BlockSpec((1, seq, HEAD), lambda bi, h: (bi, 0, off + h))
    cw = lambda off: pl.BlockSpec((3, HEAD), lambda bi, h: (0, off + h))
    in_specs = [col(0), col(DN_HEADS), col(2 * DN_HEADS),
                pl.BlockSpec((1, seq, HEAD), lambda bi, h: (bi, 0, 0)),
                pl.BlockSpec((1, seq, HEAD), lambda bi, h: (bi, 0, h)),
                cw(0), cw(DN_HEADS), cw(2 * DN_HEADS),
                pl.BlockSpec((1, HEAD), lambda bi, h: (0, 0))]
    args = [qkv3, qkv3, qkv3, gates3, zs3, conv_w, conv_w, conv_w, norm_w]
    if rope:
        in_specs += [pl.BlockSpec((seq, HEAD), lambda bi, h: (0, 0))] * 2
        args += [cos, sin]
    st_spec = pl.BlockSpec((1, 2, 1, HEAD, HEAD), lambda bi, h: (bi, 0, h, 0, 0))
    if has_s0:
        in_specs.append(st_spec)
        args.append(s0)
    out_specs = [pl.BlockSpec((1, seq, HEAD), lambda bi, h: (bi, 0, h))]
    out_shape = [jax.ShapeDtypeStruct((b, seq, DN_HEADS * HEAD), BF16)]
    if want_state:
        out_specs.append(st_spec)
        out_shape.append(jax.ShapeDtypeStruct((b, 2, DN_HEADS, HEAD, HEAD), F32))
    big = pltpu.VMEM((seq, HEAD), F32)
    scratch = [big, big, big, big, big, big, big,
               pltpu.VMEM((3 * seq, HEAD), BF16), pltpu.VMEM((3 * seq, HEAD), BF16),
               pltpu.VMEM((2 * seq, HEAD), F32), pltpu.VMEM((2 * seq, HEAD), F32),
               pltpu.VMEM((8 * n_chunks, HEAD), F32), pltpu.VMEM((8 * n_chunks, HEAD), F32),
               big, big]
    res = pl.pallas_call(
        functools.partial(_dn_kernel, seq=seq, rope=rope, has_s0=has_s0, want_state=want_state),
        grid=(b, DN_HEADS),
        in_specs=in_specs,
        out_specs=out_specs,
        out_shape=out_shape,
        scratch_shapes=scratch,
        compiler_params=_params(("parallel", "parallel")),
        name="deltanet",
    )(*args)
    return res if want_state else (res[0], None)


def _out_kernel(oa_ref, ob_ref, x_ref, mod_ref, wo1_ref, wo2_ref, gpm_ref, gpf_ref, wq_ref, keys_ref,
                x1_ref, h2_ref, l1_ref, e1_ref, r2_ref, e2_ref, sc_s):
    mod = mod_ref[0]
    gate1 = mod[:, 2 * D:3 * D]
    shift2 = mod[:, 3 * D:4 * D]
    scale2 = mod[:, 4 * D:5 * D]
    y = _dot(oa_ref[...], wo1_ref[...]) + _dot(ob_ref[...], wo2_ref[...])
    x1 = x_ref[...] + gate1 * _rms(y, gpm_ref[...])
    x1_ref[...] = x1
    h2 = (_rms(x1, gpf_ref[...]) * (1.0 + scale2) + shift2).astype(BF16)
    h2_ref[...] = pltpu.bitcast(h2, jnp.uint32)
    q = _dot(h2, wq_ref[...]).astype(BF16)
    for hp in range(2 * PEER_HEADS):
        sc_s[hp] = lax.dot_general(keys_ref[hp], q[:, hp * HEAD:(hp + 1) * HEAD], _NT,
                                   preferred_element_type=F32)
    _select(sc_s, l1_ref, e1_ref, r2_ref, e2_ref)


def _out_call(oa, ob, x2, mod3, mod_row, wo1, wo2, gpm, gpf, wq, keys):
    t = x2.shape[0]
    tm = TM_IN
    full = lambda a: pl.BlockSpec(a.shape, lambda i: (0,) * a.ndim)
    table = pl.BlockSpec((PEER_HEADS, NKEYS, tm), lambda i: (0, 0, i))
    table_pk = pl.BlockSpec((PEER_HEADS, NKEYS // 2, tm), lambda i: (0, 0, i))
    f32 = jax.ShapeDtypeStruct((PEER_HEADS, NKEYS, t), F32)
    packed = jax.ShapeDtypeStruct((PEER_HEADS, NKEYS // 2, t), jnp.uint32)
    return pl.pallas_call(
        _out_kernel,
        grid=(t // tm,),
        in_specs=[pl.BlockSpec((tm, 512), lambda i: (i, 0)),
                  pl.BlockSpec((tm, 512), lambda i: (i, 0)),
                  pl.BlockSpec((tm, D), lambda i: (i, 0)),
                  pl.BlockSpec((1, 1, 6 * D), lambda i: (mod_row(i, tm), 0, 0)),
                  full(wo1), full(wo2), full(gpm), full(gpf), full(wq), full(keys)],
        out_specs=[pl.BlockSpec((tm, D), lambda i: (i, 0)),
                   pl.BlockSpec((tm // 2, D), lambda i: (i, 0)),
                   table, table, table_pk, table_pk],
        out_shape=[jax.ShapeDtypeStruct((t, D), F32),
                   jax.ShapeDtypeStruct((t // 2, D), jnp.uint32),
                   f32, f32, packed, packed],
        scratch_shapes=[pltpu.VMEM((2 * PEER_HEADS, NKEYS, tm), F32)],
        compiler_params=_params(("parallel",)),
        name="out_proj",
    )(oa, ob, x2, mod3, wo1, wo2, gpm, gpf, wq, keys)


def _top16(x):
    vals = []
    rank = jnp.full(x.shape, float(TOPK), F32)
    for r in range(TOPK):
        m = jnp.max(x, axis=0, keepdims=True)
        vals.append(m)
        hit = x == m
        rank = jnp.where(hit, float(r), rank)
        if r + 1 < TOPK:
            x = jnp.where(hit, NEG_INF, x)
    return vals, rank


def _select(s_ref, l1_ref, e1_ref, r2_ref, e2_ref):
    tl = s_ref.shape[-1]
    row16 = lax.broadcasted_iota(jnp.int32, (TOPK, tl), 0)
    row8 = lax.broadcasted_iota(jnp.int32, (8, tl), 0)

    def head(h, carry):
        s1 = s_ref[2 * h]
        s2 = s_ref[2 * h + 1]
        a, _ = _top16(s1)
        b, rank2 = _top16(s2)
        amat = jnp.concatenate(a, axis=0)
        bmat = jnp.concatenate(b, axis=0)
        a8, b8 = amat[:8], bmat[:8]
        cands = [a[0] + bmat,
                 jnp.where(row16 >= 2, amat + b[0], NEG_INF),
                 a[1] + b8,
                 jnp.where(row8 >= 2, a8 + b[1], NEG_INF),
                 jnp.where((row8 >= 2) & (row8 <= 4), a[2] + b8, NEG_INF),
                 jnp.where((row8 >= 2) & (row8 <= 3), a[3] + b8, NEG_INF),
                 jnp.where(row8 == 2, a[4] + b8, NEG_INF)]
        work = cands
        thr = None
        for r in range(TOPK):
            m16 = jnp.max(jnp.maximum(work[0], work[1]), axis=0, keepdims=True)
            m8 = jnp.max(functools.reduce(jnp.maximum, work[2:]), axis=0, keepdims=True)
            thr = jnp.maximum(m16, m8)
            if r + 1 < TOPK:
                work = [jnp.where(w == thr, NEG_INF, w) for w in work]
        m0 = a[0] + b[0]
        z = functools.reduce(
            jnp.add, [jnp.sum(jnp.where(cd >= thr, jnp.exp(cd - m0), 0.0), axis=0, keepdims=True) for cd in cands])
        cnt = jnp.zeros_like(s1)
        for j in range(TOPK):
            cnt = cnt + jnp.where(s1 + b[j] >= thr, 1.0, 0.0)
        l1_ref[h] = cnt
        r2_ref[h] = pltpu.bitcast(rank2.astype(BF16), jnp.uint32)
        e1_ref[h] = jnp.exp(s1 - a[0]) / z
        e2_ref[h] = pltpu.bitcast(jnp.exp(s2 - b[0]).astype(BF16), jnp.uint32)
        return carry

    lax.fori_loop(0, PEER_HEADS, head, 0)


def _peer_kernel(h2_ref, u_ref, vt_ref, l1a_ref, e1a_ref, l1b_ref, e1b_ref, r2a_ref, e2a_ref, r2b_ref, e2b_ref,
                 acc_ref, at_a, at_b, ct_a, ct_b, *, n_chunks, n_items):
    s = pl.program_id(0)
    ec = u_ref.shape[0]
    tm = 2 * h2_ref.shape[0]
    pk = 16
    n_pv = NKEYS // pk
    pair = 2

    @pl.when(s == 0)
    def _():
        at_b[...] = jnp.zeros_like(at_b)
        ct_a[...] = jnp.zeros_like(ct_a)

    @pl.when((s == 0) | (lax.rem(2 * s - 2, n_chunks) == 0))
    def _():
        acc_ref[...] = jnp.zeros_like(acc_ref)

    def coefficients(c0, at_ref, ct_ref, l1_ref, e1_ref, r2_ref, e2_ref, valid):
        for lh in range(TB_PEER // HEAD):
            cols = pl.ds(pl.multiple_of(c0 + lh * HEAD, HEAD), HEAD)
            for i0 in range(0, ec // NKEYS, pair):
                bcast = lambda row: jnp.broadcast_to(row, (pk, HEAD)).astype(BF16)
                g = [[None] * n_pv for _ in range(pair)]
                for h in range(PEER_HEADS):
                    l1 = [bcast(l1_ref[h, i0 + d:i0 + d + 1, cols]) for d in range(pair)]
                    e1 = [bcast(e1_ref[h, i0 + d:i0 + d + 1, cols] * valid) for d in range(pair)]
                    for pv in range(n_pv):
                        keys2 = slice(pv * pk // 2, (pv + 1) * pk // 2)
                        r2 = pltpu.bitcast(r2_ref[h, keys2, cols], BF16)
                        e2 = pltpu.bitcast(e2_ref[h, keys2, cols], BF16)
                        for d in range(pair):
                            term = jnp.where(r2 < l1[d], e2, 0.0) * e1[d]
                            g[d][pv] = term if g[d][pv] is None else g[d][pv] + term
                for d in range(pair):
                    for pv in range(n_pv):
                        rows = slice((i0 + d) * NKEYS + pv * pk, (i0 + d) * NKEYS + (pv + 1) * pk)
                        a = at_ref[rows, cols]
                        th = jnp.tanh((a * (_GELU_C1 + _GELU_C2 * (a * a))).astype(BF16))
                        hx = 0.5 * a.astype(BF16)
                        ct_ref[rows, cols] = g[d][pv] * (hx + hx * th)

    def phase(half, at_w, at_r, ct_w, ct_r, l1_ref, e1_ref, r2_ref, e2_ref, valid):
        def body(lb, carry):
            c0 = pl.multiple_of(lb * TB_PEER, TB_PEER)
            tcols = pl.ds(c0, TB_PEER)
            h2 = pltpu.bitcast(h2_ref[pl.ds(pl.multiple_of(lb * (TB_PEER // 2), TB_PEER // 2), TB_PEER // 2), :],
                               BF16)
            u = pltpu.bitcast(u_ref[half * ec // 2:(half + 1) * ec // 2, :], BF16)
            at_w[:, tcols] = lax.dot_general(u, h2, _NT, preferred_element_type=F32)
            coefficients(c0, at_r, ct_w, l1_ref, e1_ref, r2_ref, e2_ref, valid)
            vt = pltpu.bitcast(vt_ref[:, half * ec:(half + 1) * ec], BF16)
            acc_ref[:, tcols] += _dot(vt, ct_r[:, tcols])
            return carry

        lax.fori_loop(0, tm // TB_PEER, body, 0)

    valid_a = jnp.where(s >= 1, 1.0, 0.0).astype(F32)
    valid_b = jnp.where(2 * s < n_items, 1.0, 0.0).astype(F32)
    phase(0, at_a, at_b, ct_b, ct_a, l1a_ref, e1a_ref, r2a_ref, e2a_ref, valid_a)
    phase(1, at_b, at_a, ct_a, ct_b, l1b_ref, e1b_ref, r2b_ref, e2b_ref, valid_b)


def _peer_call(h2_pk, u_pk, vt_pk, l1, e1, r2, e2, tm=TM_PEER, ec=EC_PEER):
    t = 2 * h2_pk.shape[0]
    n_exp = 2 * u_pk.shape[0]
    n = n_exp // ec
    n_tiles = t // tm
    n_items = n_tiles * n
    i1_blk = ec // NKEYS
    half = n // 2
    tile_sc = lambda s: jnp.minimum((2 * s) // n, n_tiles - 1)
    tile_a = lambda s: jnp.maximum(2 * s - 1, 0) // n
    tile_b = tile_sc
    tile_out = lambda s: jnp.maximum(2 * s - 2, 0) // n
    chunk_a = lambda s: lax.rem(2 * s - 1 + n, n)
    chunk_b = lambda s: lax.rem(2 * s, n)
    once = dict(pipeline_mode=pl.Buffered(1))
    row_a = pl.BlockSpec((PEER_HEADS, i1_blk, tm), lambda s: (0, chunk_a(s), tile_a(s)))
    row_b = pl.BlockSpec((PEER_HEADS, i1_blk, tm), lambda s: (0, chunk_b(s), tile_b(s)))
    tab_a = pl.BlockSpec((PEER_HEADS, NKEYS // 2, tm), lambda s: (0, 0, tile_a(s)), **once)
    tab_b = pl.BlockSpec((PEER_HEADS, NKEYS // 2, tm), lambda s: (0, 0, tile_b(s)), **once)
    return pl.pallas_call(
        functools.partial(_peer_kernel, n_chunks=n, n_items=n_items),
        grid=(n_items // 2 + 1,),
        in_specs=[pl.BlockSpec((tm // 2, D), lambda s: (tile_sc(s), 0)),
                  pl.BlockSpec((ec, D), lambda s: (lax.rem(s, half), 0)),
                  pl.BlockSpec((D // 2, 2 * ec), lambda s: (0, lax.rem(s + half - 1, half))),
                  row_a, row_a, row_b, row_b, tab_a, tab_a, tab_b, tab_b],
        out_specs=pl.BlockSpec((D, tm), lambda s: (0, tile_out(s))),
        out_shape=jax.ShapeDtypeStruct((D, t), F32),
        scratch_shapes=[pltpu.VMEM((ec, tm), F32), pltpu.VMEM((ec, tm), F32),
                        pltpu.VMEM((ec, tm), BF16), pltpu.VMEM((ec, tm), BF16)],
        compiler_params=_params(("arbitrary",)),
        name="peer",
    )(h2_pk, u_pk, vt_pk, l1, e1, l1, e1, r2, e2, r2, e2)


def _fin_kernel(ot_ref, x1_ref, mod_ref, gpost_ref, y_ref):
    gate2 = mod_ref[0][:, 5 * D:6 * D]
    y_ref[...] = x1_ref[...] + gate2 * _rms(ot_ref[...].T, gpost_ref[...])


def _fin_call(ot, x1, mod3, mod_row, gpost):
    t = x1.shape[0]
    tm = TM_IN
    return pl.pallas_call(
        _fin_kernel,
        grid=(t // tm,),
        in_specs=[pl.BlockSpec((D, tm), lambda i: (0, i)),
                  pl.BlockSpec((tm, D), lambda i: (i, 0)),
                  pl.BlockSpec((1, 1, 6 * D), lambda i: (mod_row(i, tm), 0, 0)),
                  pl.BlockSpec((1, D), lambda i: (0, 0))],
        out_specs=pl.BlockSpec((tm, D), lambda i: (i, 0)),
        out_shape=jax.ShapeDtypeStruct((t, D), F32),
        compiler_params=_params(("parallel",)),
        name="peer_residual",
    )(ot, x1, mod3, gpost)


def _rope_tables(seq):
    pos = jnp.arange(seq)
    quarter = HEAD // 4
    inv = ROPE_BASE ** (-jnp.arange(quarter, dtype=F32) / quarter)
    ang_r = (pos // GRID_W).astype(F32)[:, None] * inv
    ang_c = (pos % GRID_W).astype(F32)[:, None] * inv
    cos = jnp.concatenate([jnp.cos(ang_r)] * 2 + [jnp.cos(ang_c)] * 2, axis=-1)
    sin = jnp.concatenate([-jnp.sin(ang_r), jnp.sin(ang_r), -jnp.sin(ang_c), jnp.sin(ang_c)], axis=-1)
    return cos, sin


def _layer(x, mod3, mod_row, s0, rope, want_state, w):
    b, seq, _ = x.shape
    x2 = x.reshape(b * seq, D)
    oa, qkv, zs, gates = _in_call(x2, mod3, mod_row, w["gpre"], w["wgu"], w["wgv"], w["wqkv"], w["wz"], w["wab"],
                                  w["ws"], w["bs"], w["alog"], w["dtb"])
    cos, sin = _rope_tables(seq) if rope else (None, None)
    ob, st = _dn_call(qkv.reshape(b, seq, -1), gates.reshape(b, seq, HEAD), zs.reshape(b, seq, -1),
                      w["conv"], w["dnorm"], cos, sin, s0, want_state)
    x1, h2, l1, e1, r2, e2 = _out_call(oa, ob.reshape(b * seq, -1), x2, mod3, mod_row, w["wo1"], w["wo2"],
                                       w["gpm"], w["gpf"], w["wq"], w["keys"])
    ot = _peer_call(h2, w["u"], w["vt"], l1, e1, r2, e2, tm=min(TM_PEER, b * seq))
    y = _fin_call(ot, x1, mod3, mod_row, w["gpost"])
    return y.reshape(b, seq, D), st


def kernel(x_prompt, x_sample, state_delta, c, c_ctx, w_mod, b_mod, g_pre_mix, g_post_mix, g_pre_ffn, g_post_ffn,
           w_in, w_out, gm_ws, gm_bs, dn_conv, dn_a_log, dn_dt_bias, dn_norm, peer_wq, peer_keys, peer_u, peer_v):
    depth = w_mod.shape[0]
    dec_batch, dec_seq, _ = x_sample.shape
    xp, xs = x_prompt, x_sample
    states = []
    cc = jnp.zeros((16, D), F32).at[0].set(c_ctx).at[1:1 + dec_batch].set(c)
    for l in range(depth):
        mod3 = _mod_call(cc, w_mod[l], b_mod[l][None, :]).reshape(16, 1, 6 * D)
        wi = w_in[l]
        pad16 = lambda v: jnp.zeros((1, HEAD), F32).at[0, :v.size].set(v.reshape(-1))
        w = {
            "gpre": g_pre_mix[l][None, :], "gpm": g_post_mix[l][None, :], "gpf": g_pre_ffn[l][None, :],
            "gpost": g_post_ffn[l][None, :],
            "wgu": wi[:, 0:512].astype(BF16), "wgv": wi[:, 512:1024].astype(BF16),
            "wqkv": wi[:, 1024:2560].astype(BF16), "wz": wi[:, 2560:3072].astype(BF16),
            "wab": jnp.zeros((D, HEAD), F32).at[:, :16].set(wi[:, 3072:3088]).astype(BF16),
            "ws": gm_ws[l].astype(BF16),
            "bs": jnp.broadcast_to(gm_bs[l][:, :, None], (GM_HEADS, HEAD, HEAD)),
            "alog": pad16(dn_a_log[l]), "dtb": pad16(dn_dt_bias[l]),
            "conv": dn_conv[l], "dnorm": dn_norm[l][None, :],
            "wo1": w_out[l][:512].astype(BF16), "wo2": w_out[l][512:].astype(BF16),
            "wq": peer_wq[l].astype(BF16),
            "keys": peer_keys[l].reshape(2 * PEER_HEADS, NKEYS, HEAD).astype(BF16),
            "u": _pack_call(peer_u[l]), "vt": _pack_call(peer_v[l], transpose=True),
        }
        xp, s_ctx = _layer(xp, mod3, lambda i, tm: 0, None, False, True, w)
        xs, _ = _layer(xs, mod3, lambda i, tm: 1 + (i * tm) // dec_seq, state_delta[:, l], True, False, w)
        states.append(s_ctx.astype(x_prompt.dtype))
    return xp, xs, jnp.stack(states, axis=1)
```

```python
import functools
import math

import jax
import jax.numpy as jnp
from jax import lax
from jax.experimental import pallas as pl
from jax.experimental.pallas import tpu as pltpu

F32 = jnp.float32
BF16 = jnp.bfloat16
HI = lax.Precision.HIGHEST
EPS = 1e-6
NEG_INF = float("-inf")

D = 1024
GM_HEADS = 4
DN_HEADS = 4
HEAD = 128
CHUNK = 64
GROUP = 8
GRID_W = 64
ROPE_BASE = 10000.0
PEER_HEADS = 8
NKEYS = 128
TOPK = 16

TM_IN = 512
TM_PEER = 1024
EC_PEER = 1024
TB_PEER = 512
VMEM_LIMIT = 56 * 1024 * 1024

_NT = (((1,), (1,)), ((), ()))
_TN = (((0,), (0,)), ((), ()))


def _dot(a, b, precision=None):
    return jnp.dot(a, b, preferred_element_type=F32, precision=precision)


def _pack_kernel(x_ref, o_ref, *, transpose):
    x = x_ref[...]
    if transpose:
        x = x.T
    o_ref[...] = pltpu.bitcast(x.astype(BF16), jnp.uint32)


def _pack_call(x, transpose=False, rows=512):
    r, c = x.shape
    if transpose:
        out_spec, out_shape = pl.BlockSpec((c // 2, rows), lambda i: (0, i)), (c // 2, r)
    else:
        out_spec, out_shape = pl.BlockSpec((rows // 2, c), lambda i: (i, 0)), (r // 2, c)
    return pl.pallas_call(
        functools.partial(_pack_kernel, transpose=transpose),
        grid=(r // rows,),
        in_specs=[pl.BlockSpec((rows, c), lambda i: (i, 0))],
        out_specs=out_spec,
        out_shape=jax.ShapeDtypeStruct(out_shape, jnp.uint32),
        compiler_params=_params(("parallel",)),
        name="pack_t" if transpose else "pack",
    )(x)


def _split(x):
    hi = x.astype(BF16)
    return hi, (x - hi.astype(F32)).astype(BF16)


def _dot_split_rhs(m, x):
    hi, lo = _split(x)
    return _dot(m, hi) + _dot(m, lo)


def _dot_split_lhs(x, m):
    hi, lo = _split(x)
    return _dot(hi, m) + _dot(lo, m)


def _dot3(a, b):
    ah, al = _split(a)
    bh, bl = _split(b)
    return _dot(ah, bh) + (_dot(ah, bl) + _dot(al, bh))


_GELU_C1 = math.sqrt(2.0 / math.pi)
_GELU_C2 = _GELU_C1 * 0.044715


def _gelu(x):
    hx = 0.5 * x
    return hx + hx * jnp.tanh(x * (_GELU_C1 + _GELU_C2 * (x * x)))


def _silu(x):
    return x * jax.nn.sigmoid(x)


def _rms(x, gain):
    return x * lax.rsqrt(jnp.mean(x * x, axis=-1, keepdims=True) + EPS) * gain


def _params(sem):
    return pltpu.CompilerParams(dimension_semantics=sem, vmem_limit_bytes=VMEM_LIMIT)


def _mod_kernel(c_ref, w_ref, b_ref, o_ref):
    o_ref[...] = _dot(_silu(c_ref[...]), w_ref[...], HI) + b_ref[...]


def _mod_call(cc, w_mod, b_mod):
    n = w_mod.shape[1]
    bn = 1536
    return pl.pallas_call(
        _mod_kernel,
        grid=(n // bn,),
        in_specs=[pl.BlockSpec(cc.shape, lambda j: (0, 0)),
                  pl.BlockSpec((D, bn), lambda j: (0, j)),
                  pl.BlockSpec((1, bn), lambda j: (0, j))],
        out_specs=pl.BlockSpec((cc.shape[0], bn), lambda j: (0, j)),
        out_shape=jax.ShapeDtypeStruct((cc.shape[0], n), F32),
        compiler_params=_params(("arbitrary",)),
        name="mod",
    )(cc, w_mod, b_mod)


def _in_kernel(x_ref, mod_ref, gpre_ref, wgu_ref, wgv_ref, wqkv_ref, wz_ref, wab_ref, ws_ref, bs_ref,
               alog_ref, dtb_ref, oa_ref, qkv_ref, zs_ref, gates_ref):
    x = x_ref[...]
    mod = mod_ref[0]
    shift1 = mod[:, 0:D]
    scale1 = mod[:, D:2 * D]
    hb = (_rms(x, gpre_ref[...]) * (1.0 + scale1) + shift1).astype(BF16)
    gu = _gelu(_dot(hb, wgu_ref[...]))
    gv = _gelu(_dot(hb, wgv_ref[...]))
    qkv_ref[...] = _dot(hb, wqkv_ref[...])
    zs_ref[...] = _silu(_dot(hb, wz_ref[...]))
    pab = _dot(hb, wab_ref[...])
    t = pab + dtb_ref[...]
    softplus = jnp.maximum(t, 0.0) + jnp.log(1.0 + jnp.exp(-jnp.abs(t)))
    lane = lax.broadcasted_iota(jnp.int32, pab.shape, 1)
    gates_ref[...] = jnp.where(lane < 2 * DN_HEADS, -jnp.exp(alog_ref[...]) * softplus, jax.nn.sigmoid(pab))
    for hd in range(GM_HEADS):
        cols = slice(hd * HEAD, (hd + 1) * HEAD)
        v_h = gv[:, cols]
        vn = (v_h * lax.rsqrt(jnp.mean(v_h * v_h, axis=-1, keepdims=True) + EPS)).astype(BF16)
        for c in range(x.shape[0] // HEAD):
            rows = slice(c * HEAD, (c + 1) * HEAD)
            s = _dot(ws_ref[hd], vn[rows]) + bs_ref[hd]
            oa_ref[rows, cols] = (gu[rows, cols] * s).astype(BF16)


def _in_call(x2, mod3, mod_row, gpre, wgu, wgv, wqkv, wz, wab, ws, bs, alog, dtb):
    t = x2.shape[0]
    tm = TM_IN
    full = lambda a: pl.BlockSpec(a.shape, lambda i: (0,) * a.ndim)
    return pl.pallas_call(
        _in_kernel,
        grid=(t // tm,),
        in_specs=[pl.BlockSpec((tm, D), lambda i: (i, 0)),
                  pl.BlockSpec((1, 1, 6 * D), lambda i: (mod_row(i, tm), 0, 0)),
                  full(gpre), full(wgu), full(wgv), full(wqkv), full(wz), full(wab), full(ws), full(bs),
                  full(alog), full(dtb)],
        out_specs=[pl.BlockSpec((tm, 512), lambda i: (i, 0)),
                   pl.BlockSpec((tm, 1536), lambda i: (i, 0)),
                   pl.BlockSpec((tm, 512), lambda i: (i, 0)),
                   pl.BlockSpec((tm, HEAD), lambda i: (i, 0))],
        out_shape=[jax.ShapeDtypeStruct((t, 512), BF16),
                   jax.ShapeDtypeStruct((t, 1536), F32),
                   jax.ShapeDtypeStruct((t, 512), F32),
                   jax.ShapeDtypeStruct((t, HEAD), F32)],
        compiler_params=_params(("parallel",)),
        name="in_proj",
    )(x2, mod3, gpre, wgu, wgv, wqkv, wz, wab, ws, bs, alog, dtb)


def _dn_kernel(*refs, seq, rope, has_s0, want_state):
    it = iter(refs)
    q_ref, k_ref, v_ref, gates_ref, zs_ref = next(it), next(it), next(it), next(it), next(it)
    cwq_ref, cwk_ref, cwv_ref, nw_ref = next(it), next(it), next(it), next(it)
    cos_ref = next(it) if rope else None
    sin_ref = next(it) if rope else None
    s0_ref = next(it) if has_s0 else None
    ob_ref = next(it)
    st_ref = next(it) if want_state else None
    q_s, k_s, v_s, gf_s, gb_s, bf_s, bb_s, mqf_s, mqb_s, nf_s, nb_s, glf_s, glb_s, of_s, obk_s = it
    n_chunks = seq // CHUNK
    hd = pl.program_id(1)

    row = lax.broadcasted_iota(jnp.int32, (seq, HEAD), 0)
    lane = lax.broadcasted_iota(jnp.int32, (seq, HEAD), 1)

    def conv_silu(ref, cw_ref):
        x = ref[0]
        w = cw_ref[...]
        xm = jnp.where(row == 0, 0.0, pltpu.roll(x, 1, axis=0))
        xp = jnp.where(row == seq - 1, 0.0, pltpu.roll(x, seq - 1, axis=0))
        return _silu(xm * w[0:1] + x * w[1:2] + xp * w[2:3])

    def l2n(x):
        return x * lax.rsqrt(jnp.sum(x * x, axis=-1, keepdims=True) + EPS)

    def rot(x):
        if not rope:
            return x
        partner = jnp.where((lane & 32) == 0, pltpu.roll(x, 96, axis=1), pltpu.roll(x, 32, axis=1))
        return x * cos_ref[...] + partner * sin_ref[...]

    q_s[...] = rot(l2n(conv_silu(q_ref, cwq_ref))) * (HEAD ** -0.5)
    k_s[...] = rot(l2n(conv_silu(k_ref, cwk_ref)))
    v_s[...] = conv_silu(v_ref, cwv_ref)

    gates = gates_ref[0]
    sel_r = lax.broadcasted_iota(jnp.int32, (HEAD, HEAD), 0)
    for dst, col in ((gf_s, hd), (gb_s, DN_HEADS + hd), (bf_s, 2 * DN_HEADS + hd), (bb_s, 3 * DN_HEADS + hd)):
        dst[...] = _dot_split_lhs(gates, (sel_r == col).astype(BF16))

    ri = lax.broadcasted_iota(jnp.int32, (CHUNK, HEAD), 0)
    li = lax.broadcasted_iota(jnp.int32, (CHUNK, HEAD), 1)
    lj = li & (CHUNK - 1)
    is_f = li < CHUNK
    is_b = li >= CHUNK
    incl = (is_f & (ri >= lj)) | (is_b & (ri <= lj))
    strict = (is_f & (ri > lj)) | (is_b & (ri < lj))
    diag = lj == ri
    ti = lax.broadcasted_iota(jnp.int32, (CHUNK, CHUNK), 0)
    tj = lax.broadcasted_iota(jnp.int32, (CHUNK, CHUNK), 1)
    lower = (tj <= ti).astype(BF16)
    upper = (tj >= ti).astype(BF16)
    ones = jnp.ones((CHUNK, CHUNK), BF16)
    ei = lax.broadcasted_iota(jnp.int32, (2 * CHUNK, 2 * CHUNK), 0)
    ej = lax.broadcasted_iota(jnp.int32, (2 * CHUNK, 2 * CHUNK), 1)
    eye = (ei == ej).astype(F32)

    mq_rows = HEAD + CHUNK
    group = min(GROUP, n_chunks)

    def finish(c, rows, qc, kc, gcd, gxc, attn_d, wu_d, mq_s, n_s, gl_s, o_s):
        tot = jnp.sum(gxc, axis=0, keepdims=True)
        kd = (kc * jnp.exp(tot - gcd)).astype(BF16)
        wub = wu_d.astype(BF16)
        kw = lax.dot_general(kd, wub, _TN, preferred_element_type=F32)
        aw = _dot(attn_d.astype(BF16), wub)
        mq = jnp.concatenate([-kw[:, :HEAD], qc * jnp.exp(gcd) - aw[:, :HEAD]], axis=0)
        mq_s[pl.ds(pl.multiple_of(c * mq_rows, CHUNK), mq_rows), :] = mq.astype(BF16)
        n_s[pl.ds(pl.multiple_of(c * HEAD, HEAD), HEAD), :] = kw[:, HEAD:]
        o_s[rows, :] = aw[:, HEAD:]
        gl_s[pl.ds(pl.multiple_of(c * 8, 8), 8), :] = jnp.broadcast_to(jnp.exp(tot), (8, HEAD))

    def prep(gi, carry):
        cs = [gi * group + g for g in range(group)]
        rws = [pl.ds(pl.multiple_of(c * CHUNK, CHUNK), CHUNK) for c in cs]
        st = [dict(qc=q_s[r, :], kc=k_s[r, :], vc=v_s[r, :], gfc=gf_s[r, :], gbc=gb_s[r, :],
                   bfc=bf_s[r, :], bbc=bb_s[r, :]) for r in rws]
        for s in st:
            s["gc_f"] = _dot_split_rhs(lower, s["gfc"])
            s["gc_b"] = _dot_split_rhs(upper, s["gbc"])
            s["gc"] = jnp.where(is_f, s["gc_f"], s["gc_b"])
            kb = s["kc"].astype(BF16)
            kk2 = jnp.concatenate([kb, kb], axis=0)
            s["kk"] = lax.dot_general(kb, kk2, _NT, preferred_element_type=F32)
            s["qk"] = lax.dot_general(s["qc"].astype(BF16), kk2, _NT, preferred_element_type=F32)
        for s in st:
            s["gc_row"] = _dot_split_rhs(ones, jnp.where(diag, s["gc"], 0.0))
        for s in st:
            dec = jnp.exp(jnp.where(incl, s["gc"] - s["gc_row"], NEG_INF))
            beta = jnp.where(is_f, s["bfc"], s["bbc"])
            a = jnp.where(strict, beta * s["kk"] * dec, 0.0)
            s["x"] = -jnp.concatenate([jnp.where(is_f, a, 0.0), jnp.where(is_f, 0.0, a)], axis=0)
            s["p"] = eye + s["x"]
            s["attn"] = s["qk"] * dec
        for it in range(5):
            mm = _dot3 if it < 3 else (lambda a_, b_: _dot(a_.astype(BF16), b_.astype(BF16)))
            for s in st:
                s["x"] = mm(s["x"], s["x"])
            for s in st:
                s["p"] = s["p"] + mm(s["p"], s["x"])
        for s in st:
            kc, vc, bfc, bbc = s["kc"], s["vc"], s["bfc"], s["bbc"]
            rhs = jnp.concatenate(
                [jnp.concatenate([kc * bfc * jnp.exp(s["gc_f"]), vc * bfc], axis=1),
                 jnp.concatenate([kc * bbc * jnp.exp(s["gc_b"]), vc * bbc], axis=1)], axis=0)
            s["wu"] = _dot(s["p"].astype(BF16), rhs.astype(BF16))
        for s, c, r in zip(st, cs, rws):
            finish(c, r, s["qc"], s["kc"], s["gc_f"], s["gfc"], s["attn"][:, :CHUNK], s["wu"][:CHUNK],
                   mqf_s, nf_s, glf_s, of_s)
            finish(c, r, s["qc"], s["kc"], s["gc_b"], s["gbc"], pltpu.roll(s["attn"], CHUNK, axis=1)[:, :CHUNK],
                   s["wu"][CHUNK:], mqb_s, nb_s, glb_s, obk_s)
        return carry

    lax.fori_loop(0, n_chunks // group, prep, 0)

    def one_dir(c, s, mq_s, n_s, gl_s, o_s):
        rows = pl.ds(pl.multiple_of(c * CHUNK, CHUNK), CHUNK)
        r = _dot(mq_s[pl.ds(pl.multiple_of(c * mq_rows, CHUNK), mq_rows), :], s.astype(BF16))
        o_s[rows, :] += r[HEAD:]
        gl = gl_s[pl.ds(pl.multiple_of(c * 8, 8), 1), :]
        return s * gl + (r[:HEAD] + n_s[pl.ds(pl.multiple_of(c * HEAD, HEAD), HEAD), :])

    def scan(i, carry):
        s_f, s_b = carry
        s_f = one_dir(i, s_f, mqf_s, nf_s, glf_s, of_s)
        s_b = one_dir(n_chunks - 1 - i, s_b, mqb_s, nb_s, glb_s, obk_s)
        return s_f, s_b

    if has_s0:
        init = (s0_ref[0, 0, 0], s0_ref[0, 1, 0])
    else:
        init = (jnp.zeros((HEAD, HEAD), F32), jnp.zeros((HEAD, HEAD), F32))
    s_f, s_b = lax.fori_loop(0, n_chunks, scan, init)
    if want_state:
        st_ref[0, 0, 0] = s_f
        st_ref[0, 1, 0] = s_b

    o = of_s[...] + obk_s[...]
    ob_ref[0] = (_rms(o, nw_ref[...]) * zs_ref[0]).astype(BF16)


def _dn_call(qkv3, gates3, zs3, conv_w, norm_w, cos, sin, s0, want_state):
    b, seq, _ = qkv3.shape
    rope = cos is not None
    has_s0 = s0 is not None
    n_chunks = seq // CHUNK
    col = lambda off: pl.BlockSpec((1, seq, HEAD), lambda bi, h: (bi, 0, off + h))
    cw = lambda off: pl.BlockSpec((3, HEAD), lambda bi, h: (0, off + h))
    in_specs = [col(0), col(DN_HEADS), col(2 * DN_HEADS),
                pl.BlockSpec((1, seq, HEAD), lambda bi, h: (bi, 0, 0)),
                pl.BlockSpec((1, seq, HEAD), lambda bi, h: (bi, 0, h)),
                cw(0), cw(DN_HEADS), cw(2 * DN_HEADS),
                pl.BlockSpec((1, HEAD), lambda bi, h: (0, 0))]
    args = [qkv3, qkv3, qkv3, gates3, zs3, conv_w, conv_w, conv_w, norm_w]
    if rope:
        in_specs += [pl.BlockSpec((seq, HEAD), lambda bi, h: (0, 0))] * 2
        args += [cos, sin]
    st_spec = pl.BlockSpec((1, 2, 1, HEAD, HEAD), lambda bi, h: (bi, 0, h, 0, 0))
    if has_s0:
        in_specs.append(st_spec)
        args.append(s0)
    out_specs = [pl.BlockSpec((1, seq, HEAD), lambda bi, h: (bi, 0, h))]
    out_shape = [jax.ShapeDtypeStruct((b, seq, DN_HEADS * HEAD), BF16)]
    if want_state:
        out_specs.append(st_spec)
        out_shape.append(jax.ShapeDtypeStruct((b, 2, DN_HEADS, HEAD, HEAD), F32))
    big = pltpu.VMEM((seq, HEAD), F32)
    scratch = [big, big, big, big, big, big, big,
               pltpu.VMEM((3 * seq, HEAD), BF16), pltpu.VMEM((3 * seq, HEAD), BF16),
               pltpu.VMEM((2 * seq, HEAD), F32), pltpu.VMEM((2 * seq, HEAD), F32),
               pltpu.VMEM((8 * n_chunks, HEAD), F32), pltpu.VMEM((8 * n_chunks, HEAD), F32),
               big, big]
    res = pl.pallas_call(
        functools.partial(_dn_kernel, seq=seq, rope=rope, has_s0=has_s0, want_state=want_state),
        grid=(b, DN_HEADS),
        in_specs=in_specs,
        out_specs=out_specs,
        out_shape=out_shape,
        scratch_shapes=scratch,
        compiler_params=_params(("parallel", "parallel")),
        name="deltanet",
    )(*args)
    return res if want_state else (res[0], None)


def _out_kernel(oa_ref, ob_ref, x_ref, mod_ref, wo1_ref, wo2_ref, gpm_ref, gpf_ref, wq_ref, keys_ref,
                x1_ref, h2_ref, l1_ref, e1_ref, r2_ref, e2_ref, sc_s):
    mod = mod_ref[0]
    gate1 = mod[:, 2 * D:3 * D]
    shift2 = mod[:, 3 * D:4 * D]
    scale2 = mod[:, 4 * D:5 * D]
    y = _dot(oa_ref[...], wo1_ref[...]) + _dot(ob_ref[...], wo2_ref[...])
    x1 = x_ref[...] + gate1 * _rms(y, gpm_ref[...])
    x1_ref[...] = x1
    h2 = (_rms(x1, gpf_ref[...]) * (1.0 + scale2) + shift2).astype(BF16)
    h2_ref[...] = pltpu.bitcast(h2, jnp.uint32)
    q = _dot(h2, wq_ref[...]).astype(BF16)
    for hp in range(2 * PEER_HEADS):
        sc_s[hp] = lax.dot_general(keys_ref[hp], q[:, hp * HEAD:(hp + 1) * HEAD], _NT,
                                   preferred_element_type=F32)
    _select(sc_s, l1_ref, e1_ref, r2_ref, e2_ref)


def _out_call(oa, ob, x2, mod3, mod_row, wo1, wo2, gpm, gpf, wq, keys):
    t = x2.shape[0]
    tm = TM_IN
    full = lambda a: pl.BlockSpec(a.shape, lambda i: (0,) * a.ndim)
    table = pl.BlockSpec((PEER_HEADS, NKEYS, tm), lambda i: (0, 0, i))
    table_pk = pl.BlockSpec((PEER_HEADS, NKEYS // 2, tm), lambda i: (0, 0, i))
    f32 = jax.ShapeDtypeStruct((PEER_HEADS, NKEYS, t), F32)
    packed = jax.ShapeDtypeStruct((PEER_HEADS, NKEYS // 2, t), jnp.uint32)
    return pl.pallas_call(
        _out_kernel,
        grid=(t // tm,),
        in_specs=[pl.BlockSpec((tm, 512), lambda i: (i, 0)),
                  pl.BlockSpec((tm, 512), lambda i: (i, 0)),
                  pl.BlockSpec((tm, D), lambda i: (i, 0)),
                  pl.BlockSpec((1, 1, 6 * D), lambda i: (mod_row(i, tm), 0, 0)),
                  full(wo1), full(wo2), full(gpm), full(gpf), full(wq), full(keys)],
        out_specs=[pl.BlockSpec((tm, D), lambda i: (i, 0)),
                   pl.BlockSpec((tm // 2, D), lambda i: (i, 0)),
                   table, table, table_pk, table_pk],
        out_shape=[jax.ShapeDtypeStruct((t, D), F32),
                   jax.ShapeDtypeStruct((t // 2, D), jnp.uint32),
                   f32, f32, packed, packed],
        scratch_shapes=[pltpu.VMEM((2 * PEER_HEADS, NKEYS, tm), F32)],
        compiler_params=_params(("parallel",)),
        name="out_proj",
    )(oa, ob, x2, mod3, wo1, wo2, gpm, gpf, wq, keys)


def _top16(x):
    vals = []
    rank = jnp.full(x.shape, float(TOPK), F32)
    for r in range(TOPK):
        m = jnp.max(x, axis=0, keepdims=True)
        vals.append(m)
        hit = x == m
        rank = jnp.where(hit, float(r), rank)
        if r + 1 < TOPK:
            x = jnp.where(hit, NEG_INF, x)
    return vals, rank


def _select(s_ref, l1_ref, e1_ref, r2_ref, e2_ref):
    tl = s_ref.shape[-1]
    row16 = lax.broadcasted_iota(jnp.int32, (TOPK, tl), 0)
    row8 = lax.broadcasted_iota(jnp.int32, (8, tl), 0)

    def head(h, carry):
        s1 = s_ref[2 * h]
        s2 = s_ref[2 * h + 1]
        a, _ = _top16(s1)
        b, rank2 = _top16(s2)
        amat = jnp.concatenate(a, axis=0)
        bmat = jnp.concatenate(b, axis=0)
        a8, b8 = amat[:8], bmat[:8]
        cands = [a[0] + bmat,
                 jnp.where(row16 >= 2, amat + b[0], NEG_INF),
                 a[1] + b8,
                 jnp.where(row8 >= 2, a8 + b[1], NEG_INF),
                 jnp.where((row8 >= 2) & (row8 <= 4), a[2] + b8, NEG_INF),
                 jnp.where((row8 >= 2) & (row8 <= 3), a[3] + b8, NEG_INF),
                 jnp.where(row8 == 2, a[4] + b8, NEG_INF)]
        work = cands
        thr = None
        for r in range(TOPK):
            m16 = jnp.max(jnp.maximum(work[0], work[1]), axis=0, keepdims=True)
            m8 = jnp.max(functools.reduce(jnp.maximum, work[2:]), axis=0, keepdims=True)
            thr = jnp.maximum(m16, m8)
            if r + 1 < TOPK:
                work = [jnp.where(w == thr, NEG_INF, w) for w in work]
        m0 = a[0] + b[0]
        z = functools.reduce(
            jnp.add, [jnp.sum(jnp.where(cd >= thr, jnp.exp(cd - m0), 0.0), axis=0, keepdims=True) for cd in cands])
        cnt = jnp.zeros_like(s1)
        for j in range(TOPK):
            cnt = cnt + jnp.where(s1 + b[j] >= thr, 1.0, 0.0)
        l1_ref[h] = cnt
        r2_ref[h] = pltpu.bitcast(rank2.astype(BF16), jnp.uint32)
        e1_ref[h] = jnp.exp(s1 - a[0]) / z
        e2_ref[h] = pltpu.bitcast(jnp.exp(s2 - b[0]).astype(BF16), jnp.uint32)
        return carry

    lax.fori_loop(0, PEER_HEADS, head, 0)


def _peer_kernel(h2_ref, u_ref, vt_ref, l1a_ref, e1a_ref, l1b_ref, e1b_ref, r2a_ref, e2a_ref, r2b_ref, e2b_ref,
                 acc_ref, at_a, at_b, ct_a, ct_b, *, n_chunks, n_items):
    s = pl.program_id(0)
    ec = u_ref.shape[0]
    tm = 2 * h2_ref.shape[0]
    pk = 16
    n_pv = NKEYS // pk
    pair = 2

    @pl.when(s == 0)
    def _():
        at_b[...] = jnp.zeros_like(at_b)
        ct_a[...] = jnp.zeros_like(ct_a)

    @pl.when((s == 0) | (lax.rem(2 * s - 2, n_chunks) == 0))
    def _():
        acc_ref[...] = jnp.zeros_like(acc_ref)

    def coefficients(c0, at_ref, ct_ref, l1_ref, e1_ref, r2_ref, e2_ref, valid):
        for lh in range(TB_PEER // HEAD):
            cols = pl.ds(pl.multiple_of(c0 + lh * HEAD, HEAD), HEAD)
            for i0 in range(0, ec // NKEYS, pair):
                bcast = lambda row: jnp.broadcast_to(row, (pk, HEAD)).astype(BF16)
                g = [[None] * n_pv for _ in range(pair)]
                for h in range(PEER_HEADS):
                    l1 = [bcast(l1_ref[h, i0 + d:i0 + d + 1, cols]) for d in range(pair)]
                    e1 = [bcast(e1_ref[h, i0 + d:i0 + d + 1, cols] * valid) for d in range(pair)]
                    for pv in range(n_pv):
                        keys2 = slice(pv * pk // 2, (pv + 1) * pk // 2)
                        r2 = pltpu.bitcast(r2_ref[h, keys2, cols], BF16)
                        e2 = pltpu.bitcast(e2_ref[h, keys2, cols], BF16)
                        for d in range(pair):
                            term = jnp.where(r2 < l1[d], e2, 0.0) * e1[d]
                            g[d][pv] = term if g[d][pv] is None else g[d][pv] + term
                for d in range(pair):
                    for pv in range(n_pv):
                        rows = slice((i0 + d) * NKEYS + pv * pk, (i0 + d) * NKEYS + (pv + 1) * pk)
                        a = at_ref[rows, cols]
                        th = jnp.tanh((a * (_GELU_C1 + _GELU_C2 * (a * a))).astype(BF16))
                        hx = 0.5 * a.astype(BF16)
                        ct_ref[rows, cols] = g[d][pv] * (hx + hx * th)

    def phase(half, at_w, at_r, ct_w, ct_r, l1_ref, e1_ref, r2_ref, e2_ref, valid):
        def body(lb, carry):
            c0 = pl.multiple_of(lb * TB_PEER, TB_PEER)
            tcols = pl.ds(c0, TB_PEER)
            h2 = pltpu.bitcast(h2_ref[pl.ds(pl.multiple_of(lb * (TB_PEER // 2), TB_PEER // 2), TB_PEER // 2), :],
                               BF16)
            u = pltpu.bitcast(u_ref[half * ec // 2:(half + 1) * ec // 2, :], BF16)
            at_w[:, tcols] = lax.dot_general(u, h2, _NT, preferred_element_type=F32)
            coefficients(c0, at_r, ct_w, l1_ref, e1_ref, r2_ref, e2_ref, valid)
            vt = pltpu.bitcast(vt_ref[:, half * ec:(half + 1) * ec], BF16)
            acc_ref[:, tcols] += _dot(vt, ct_r[:, tcols])
            return carry

        lax.fori_loop(0, tm // TB_PEER, body, 0)

    valid_a = jnp.where(s >= 1, 1.0, 0.0).astype(F32)
    valid_b = jnp.where(2 * s < n_items, 1.0, 0.0).astype(F32)
    phase(0, at_a, at_b, ct_b, ct_a, l1a_ref, e1a_ref, r2a_ref, e2a_ref, valid_a)
    phase(1, at_b, at_a, ct_a, ct_b, l1b_ref, e1b_ref, r2b_ref, e2b_ref, valid_b)


def _peer_call(h2_pk, u_pk, vt_pk, l1, e1, r2, e2, tm=TM_PEER, ec=EC_PEER):
    t = 2 * h2_pk.shape[0]
    n_exp = 2 * u_pk.shape[0]
    n = n_exp // ec
    n_tiles = t // tm
    n_items = n_tiles * n
    i1_blk = ec // NKEYS
    half = n // 2
    tile_sc = lambda s: jnp.minimum((2 * s) // n, n_tiles - 1)
    tile_a = lambda s: jnp.maximum(2 * s - 1, 0) // n
    tile_b = tile_sc
    tile_out = lambda s: jnp.maximum(2 * s - 2, 0) // n
    chunk_a = lambda s: lax.rem(2 * s - 1 + n, n)
    chunk_b = lambda s: lax.rem(2 * s, n)
    once = dict(pipeline_mode=pl.Buffered(1))
    row_a = pl.BlockSpec((PEER_HEADS, i1_blk, tm), lambda s: (0, chunk_a(s), tile_a(s)))
    row_b = pl.BlockSpec((PEER_HEADS, i1_blk, tm), lambda s: (0, chunk_b(s), tile_b(s)))
    tab_a = pl.BlockSpec((PEER_HEADS, NKEYS // 2, tm), lambda s: (0, 0, tile_a(s)), **once)
    tab_b = pl.BlockSpec((PEER_HEADS, NKEYS // 2, tm), lambda s: (0, 0, tile_b(s)), **once)
    return pl.pallas_call(
        functools.partial(_peer_kernel, n_chunks=n, n_items=n_items),
        grid=(n_items // 2 + 1,),
        in_specs=[pl.BlockSpec((tm // 2, D), lambda s: (tile_sc(s), 0)),
                  pl.BlockSpec((ec, D), lambda s: (lax.rem(s, half), 0)),
                  pl.BlockSpec((D // 2, 2 * ec), lambda s: (0, lax.rem(s + half - 1, half))),
                  row_a, row_a, row_b, row_b, tab_a, tab_a, tab_b, tab_b],
        out_specs=pl.BlockSpec((D, tm), lambda s: (0, tile_out(s))),
        out_shape=jax.ShapeDtypeStruct((D, t), F32),
        scratch_shapes=[pltpu.VMEM((ec, tm), F32), pltpu.VMEM((ec, tm), F32),
                        pltpu.VMEM((ec, tm), BF16), pltpu.VMEM((ec, tm), BF16)],
        compiler_params=_params(("arbitrary",)),
        name="peer",
    )(h2_pk, u_pk, vt_pk, l1, e1, l1, e1, r2, e2, r2, e2)


def _fin_kernel(ot_ref, x1_ref, mod_ref, gpost_ref, y_ref):
    gate2 = mod_ref[0][:, 5 * D:6 * D]
    y_ref[...] = x1_ref[...] + gate2 * _rms(ot_ref[...].T, gpost_ref[...])


def _fin_call(ot, x1, mod3, mod_row, gpost):
    t = x1.shape[0]
    tm = TM_IN
    return pl.pallas_call(
        _fin_kernel,
        grid=(t // tm,),
        in_specs=[pl.BlockSpec((D, tm), lambda i: (0, i)),
                  pl.BlockSpec((tm, D), lambda i: (i, 0)),
                  pl.BlockSpec((1, 1, 6 * D), lambda i: (mod_row(i, tm), 0, 0)),
                  pl.BlockSpec((1, D), lambda i: (0, 0))],
        out_specs=pl.BlockSpec((tm, D), lambda i: (i, 0)),
        out_shape=jax.ShapeDtypeStruct((t, D), F32),
        compiler_params=_params(("parallel",)),
        name="peer_residual",
    )(ot, x1, mod3, gpost)


def _rope_tables(seq):
    pos = jnp.arange(seq)
    quarter = HEAD // 4
    inv = ROPE_BASE ** (-jnp.arange(quarter, dtype=F32) / quarter)
    ang_r = (pos // GRID_W).astype(F32)[:, None] * inv
    ang_c = (pos % GRID_W).astype(F32)[:, None] * inv
    cos = jnp.concatenate([jnp.cos(ang_r)] * 2 + [jnp.cos(ang_c)] * 2, axis=-1)
    sin = jnp.concatenate([-jnp.sin(ang_r), jnp.sin(ang_r), -jnp.sin(ang_c), jnp.sin(ang_c)], axis=-1)
    return cos, sin


def _layer(x, mod3, mod_row, s0, rope, want_state, w):
    b, seq, _ = x.shape
    x2 = x.reshape(b * seq, D)
    oa, qkv, zs, gates = _in_call(x2, mod3, mod_row, w["gpre"], w["wgu"], w["wgv"], w["wqkv"], w["wz"], w["wab"],
                                  w["ws"], w["bs"], w["alog"], w["dtb"])
    cos, sin = _rope_tables(seq) if rope else (None, None)
    ob, st = _dn_call(qkv.reshape(b, seq, -1), gates.reshape(b, seq, HEAD), zs.reshape(b, seq, -1),
                      w["conv"], w["dnorm"], cos, sin, s0, want_state)
    x1, h2, l1, e1, r2, e2 = _out_call(oa, ob.reshape(b * seq, -1), x2, mod3, mod_row, w["wo1"], w["wo2"],
                                       w["gpm"], w["gpf"], w["wq"], w["keys"])
    ot = _peer_call(h2, w["u"], w["vt"], l1, e1, r2, e2, tm=min(TM_PEER, b * seq))
    y = _fin_call(ot, x1, mod3, mod_row, w["gpost"])
    return y.reshape(b, seq, D), st


def kernel(x_prompt, x_sample, state_delta, c, c_ctx, w_mod, b_mod, g_pre_mix, g_post_mix, g_pre_ffn, g_post_ffn,
           w_in, w_out, gm_ws, gm_bs, dn_conv, dn_a_log, dn_dt_bias, dn_norm, peer_wq, peer_keys, peer_u, peer_v):
    depth = w_mod.shape[0]
    dec_batch, dec_seq, _ = x_sample.shape
    xp, xs = x_prompt, x_sample
    states = []
    cc = jnp.zeros((16, D), F32).at[0].set(c_ctx).at[1:1 + dec_batch].set(c)
    for l in range(depth):
        mod3 = _mod_call(cc, w_mod[l], b_mod[l][None, :]).reshape(16, 1, 6 * D)
        wi = w_in[l]
        pad16 = lambda v: jnp.zeros((1, HEAD), F32).at[0, :v.size].set(v.reshape(-1))
        w = {
            "gpre": g_pre_mix[l][None, :], "gpm": g_post_mix[l][None, :], "gpf": g_pre_ffn[l][None, :],
            "gpost": g_post_ffn[l][None, :],
            "wgu": wi[:, 0:512].astype(BF16), "wgv": wi[:, 512:1024].astype(BF16),
            "wqkv": wi[:, 1024:2560].astype(BF16), "wz": wi[:, 2560:3072].astype(BF16),
            "wab": jnp.zeros((D, HEAD), F32).at[:, :16].set(wi[:, 3072:3088]).astype(BF16),
            "ws": gm_ws[l].astype(BF16),
            "bs": jnp.broadcast_to(gm_bs[l][:, :, None], (GM_HEADS, HEAD, HEAD)),
            "alog": pad16(dn_a_log[l]), "dtb": pad16(dn_dt_bias[l]),
            "conv": dn_conv[l], "dnorm": dn_norm[l][None, :],
            "wo1": w_out[l][:512].astype(BF16), "wo2": w_out[l][512:].astype(BF16),
            "wq": peer_wq[l].astype(BF16),
            "keys": peer_keys[l].reshape(2 * PEER_HEADS, NKEYS, HEAD).astype(BF16),
            "u": _pack_call(peer_u[l]), "vt": _pack_call(peer_v[l], transpose=True),
        }
        xp, s_ctx = _layer(xp, mod3, lambda i, tm: 0, None, False, True, w)
        xs, _ = _layer(xs, mod3, lambda i, tm: 1 + (i * tm) // dec_seq, state_delta[:, l], True, False, w)
        states.append(s_ctx.astype(x_prompt.dtype))
    return xp, xs, jnp.stack(states, axis=1)
```

```python
import functools
import math

import jax
import jax.numpy as jnp
from jax import lax
from jax.experimental import pallas as pl
from jax.experimental.pallas import tpu as pltpu

F32 = jnp.float32
BF16 = jnp.bfloat16
HI = lax.Precision.HIGHEST
EPS = 1e-6
NEG_INF = float("-inf")

D = 1024
GM_HEADS = 4
DN_HEADS = 4
HEAD = 128
CHUNK = 64
GROUP = 8
GRID_W = 64
ROPE_BASE = 10000.0
PEER_HEADS = 8
NKEYS = 128
TOPK = 16

TM_IN = 512
TM_PEER = 1024
EC_PEER = 1024
TB_PEER = 512
VMEM_LIMIT = 56 * 1024 * 1024

_NT = (((1,), (1,)), ((), ()))
_TN = (((0,), (0,)), ((), ()))


def _dot(a, b, precision=None):
    return jnp.dot(a, b, preferred_element_type=F32, precision=precision)


def _pack_kernel(x_ref, o_ref, *, transpose):
    x = x_ref[...]
    if transpose:
        x = x.T
    o_ref[...] = pltpu.bitcast(x.astype(BF16), jnp.uint32)


def _pack_call(x, transpose=False, rows=512):
    r, c = x.shape
    if transpose:
        out_spec, out_shape = pl.BlockSpec((c // 2, rows), lambda i: (0, i)), (c // 2, r)
    else:
        out_spec, out_shape = pl.BlockSpec((rows // 2, c), lambda i: (i, 0)), (r // 2, c)
    return pl.pallas_call(
        functools.partial(_pack_kernel, transpose=transpose),
        grid=(r // rows,),
        in_specs=[pl.BlockSpec((rows, c), lambda i: (i, 0))],
        out_specs=out_spec,
        out_shape=jax.ShapeDtypeStruct(out_shape, jnp.uint32),
        compiler_params=_params(("parallel",)),
        name="pack_t" if transpose else "pack",
    )(x)


def _split(x):
    hi = x.astype(BF16)
    return hi, (x - hi.astype(F32)).astype(BF16)


def _dot_split_rhs(m, x):
    hi, lo = _split(x)
    return _dot(m, hi) + _dot(m, lo)


def _dot_split_lhs(x, m):
    hi, lo = _split(x)
    return _dot(hi, m) + _dot(lo, m)


def _dot3(a, b):
    ah, al = _split(a)
    bh, bl = _split(b)
    return _dot(ah, bh) + (_dot(ah, bl) + _dot(al, bh))


_GELU_C1 = math.sqrt(2.0 / math.pi)
_GELU_C2 = _GELU_C1 * 0.044715


def _gelu(x):
    hx = 0.5 * x
    return hx + hx * jnp.tanh(x * (_GELU_C1 + _GELU_C2 * (x * x)))


def _silu(x):
    return x * jax.nn.sigmoid(x)


def _rms(x, gain):
    return x * lax.rsqrt(jnp.mean(x * x, axis=-1, keepdims=True) + EPS) * gain


def _params(sem):
    return pltpu.CompilerParams(dimension_semantics=sem, vmem_limit_bytes=VMEM_LIMIT)


def _mod_kernel(c_ref, w_ref, b_ref, o_ref):
    o_ref[...] = _dot(_silu(c_ref[...]), w_ref[...], HI) + b_ref[...]


def _mod_call(cc, w_mod, b_mod):
    n = w_mod.shape[1]
    bn = 1536
    return pl.pallas_call(
        _mod_kernel,
        grid=(n // bn,),
        in_specs=[pl.BlockSpec(cc.shape, lambda j: (0, 0)),
                  pl.BlockSpec((D, bn), lambda j: (0, j)),
                  pl.BlockSpec((1, bn), lambda j: (0, j))],
        out_specs=pl.BlockSpec((cc.shape[0], bn), lambda j: (0, j)),
        out_shape=jax.ShapeDtypeStruct((cc.shape[0], n), F32),
        compiler_params=_params(("arbitrary",)),
        name="mod",
    )(cc, w_mod, b_mod)


def _in_kernel(x_ref, mod_ref, gpre_ref, wgu_ref, wgv_ref, wqkv_ref, wz_ref, wab_ref, ws_ref, bs_ref,
               alog_ref, dtb_ref, oa_ref, qkv_ref, zs_ref, gates_ref):
    x = x_ref[...]
    mod = mod_ref[0]
    shift1 = mod[:, 0:D]
    scale1 = mod[:, D:2 * D]
    hb = (_rms(x, gpre_ref[...]) * (1.0 + scale1) + shift1).astype(BF16)
    gu = _gelu(_dot(hb, wgu_ref[...]))
    gv = _gelu(_dot(hb, wgv_ref[...]))
    qkv_ref[...] = _dot(hb, wqkv_ref[...])
    zs_ref[...] = _silu(_dot(hb, wz_ref[...]))
    pab = _dot(hb, wab_ref[...])
    t = pab + dtb_ref[...]
    softplus = jnp.maximum(t, 0.0) + jnp.log(1.0 + jnp.exp(-jnp.abs(t)))
    lane = lax.broadcasted_iota(jnp.int32, pab.shape, 1)
    gates_ref[...] = jnp.where(lane < 2 * DN_HEADS, -jnp.exp(alog_ref[...]) * softplus, jax.nn.sigmoid(pab))
    for hd in range(GM_HEADS):
        cols = slice(hd * HEAD, (hd + 1) * HEAD)
        v_h = gv[:, cols]
        vn = (v_h * lax.rsqrt(jnp.mean(v_h * v_h, axis=-1, keepdims=True) + EPS)).astype(BF16)
        for c in range(x.shape[0] // HEAD):
            rows = slice(c * HEAD, (c + 1) * HEAD)
            s = _dot(ws_ref[hd], vn[rows]) + bs_ref[hd]
            oa_ref[rows, cols] = (gu[rows, cols] * s).astype(BF16)


def _in_call(x2, mod3, mod_row, gpre, wgu, wgv, wqkv, wz, wab, ws, bs, alog, dtb):
    t = x2.shape[0]
    tm = TM_IN
    full = lambda a: pl.BlockSpec(a.shape, lambda i: (0,) * a.ndim)
    return pl.pallas_call(
        _in_kernel,
        grid=(t // tm,),
        in_specs=[pl.BlockSpec((tm, D), lambda i: (i, 0)),
                  pl.BlockSpec((1, 1, 6 * D), lambda i: (mod_row(i, tm), 0, 0)),
                  full(gpre), full(wgu), full(wgv), full(wqkv), full(wz), full(wab), full(ws), full(bs),
                  full(alog), full(dtb)],
        out_specs=[pl.BlockSpec((tm, 512), lambda i: (i, 0)),
                   pl.BlockSpec((tm, 1536), lambda i: (i, 0)),
                   pl.BlockSpec((tm, 512), lambda i: (i, 0)),
                   pl.BlockSpec((tm, HEAD), lambda i: (i, 0))],
        out_shape=[jax.ShapeDtypeStruct((t, 512), BF16),
                   jax.ShapeDtypeStruct((t, 1536), F32),
                   jax.ShapeDtypeStruct((t, 512), F32),
                   jax.ShapeDtypeStruct((t, HEAD), F32)],
        compiler_params=_params(("parallel",)),
        name="in_proj",
    )(x2, mod3, gpre, wgu, wgv, wqkv, wz, wab, ws, bs, alog, dtb)


def _dn_kernel(*refs, seq, rope, has_s0, want_state):
    it = iter(refs)
    q_ref, k_ref, v_ref, gates_ref, zs_ref = next(it), next(it), next(it), next(it), next(it)
    cwq_ref, cwk_ref, cwv_ref, nw_ref = next(it), next(it), next(it), next(it)
    cos_ref = next(it) if rope else None
    sin_ref = next(it) if rope else None
    s0_ref = next(it) if has_s0 else None
    ob_ref = next(it)
    st_ref = next(it) if want_state else None
    q_s, k_s, v_s, gf_s, gb_s, bf_s, bb_s, mqf_s, mqb_s, nf_s, nb_s, glf_s, glb_s, of_s, obk_s = it
    n_chunks = seq // CHUNK
    hd = pl.program_id(1)

    row = lax.broadcasted_iota(jnp.int32, (seq, HEAD), 0)
    lane = lax.broadcasted_iota(jnp.int32, (seq, HEAD), 1)

    def conv_silu(ref, cw_ref):
        x = ref[0]
        w = cw_ref[...]
        xm = jnp.where(row == 0, 0.0, pltpu.roll(x, 1, axis=0))
        xp = jnp.where(row == seq - 1, 0.0, pltpu.roll(x, seq - 1, axis=0))
        return _silu(xm * w[0:1] + x * w[1:2] + xp * w[2:3])

    def l2n(x):
        return x * lax.rsqrt(jnp.sum(x * x, axis=-1, keepdims=True) + EPS)

    def rot(x):
        if not rope:
            return x
        partner = jnp.where((lane & 32) == 0, pltpu.roll(x, 96, axis=1), pltpu.roll(x, 32, axis=1))
        return x * cos_ref[...] + partner * sin_ref[...]

    q_s[...] = rot(l2n(conv_silu(q_ref, cwq_ref))) * (HEAD ** -0.5)
    k_s[...] = rot(l2n(conv_silu(k_ref, cwk_ref)))
    v_s[...] = conv_silu(v_ref, cwv_ref)

    gates = gates_ref[0]
    sel_r = lax.broadcasted_iota(jnp.int32, (HEAD, HEAD), 0)
    for dst, col in ((gf_s, hd), (gb_s, DN_HEADS + hd), (bf_s, 2 * DN_HEADS + hd), (bb_s, 3 * DN_HEADS + hd)):
        dst[...] = _dot_split_lhs(gates, (sel_r == col).astype(BF16))

    ri = lax.broadcasted_iota(jnp.int32, (CHUNK, HEAD), 0)
    li = lax.broadcasted_iota(jnp.int32, (CHUNK, HEAD), 1)
    lj = li & (CHUNK - 1)
    is_f = li < CHUNK
    is_b = li >= CHUNK
    incl = (is_f & (ri >= lj)) | (is_b & (ri <= lj))
    strict = (is_f & (ri > lj)) | (is_b & (ri < lj))
    diag = lj == ri
    ti = lax.broadcasted_iota(jnp.int32, (CHUNK, CHUNK), 0)
    tj = lax.broadcasted_iota(jnp.int32, (CHUNK, CHUNK), 1)
    lower = (tj <= ti).astype(BF16)
    upper = (tj >= ti).astype(BF16)
    ones = jnp.ones((CHUNK, CHUNK), BF16)
    ei = lax.broadcasted_iota(jnp.int32, (2 * CHUNK, 2 * CHUNK), 0)
    ej = lax.broadcasted_iota(jnp.int32, (2 * CHUNK, 2 * CHUNK), 1)
    eye = (ei == ej).astype(F32)

    mq_rows = HEAD + CHUNK
    group = min(GROUP, n_chunks)

    def finish(c, rows, qc, kc, gcd, gxc, attn_d, wu_d, mq_s, n_s, gl_s, o_s):
        tot = jnp.sum(gxc, axis=0, keepdims=True)
        kd = (kc * jnp.exp(tot - gcd)).astype(BF16)
        wub = wu_d.astype(BF16)
        kw = lax.dot_general(kd, wub, _TN, preferred_element_type=F32)
        aw = _dot(attn_d.astype(BF16), wub)
        mq = jnp.concatenate([-kw[:, :HEAD], qc * jnp.exp(gcd) - aw[:, :HEAD]], axis=0)
        mq_s[pl.ds(pl.multiple_of(c * mq_rows, CHUNK), mq_rows), :] = mq.astype(BF16)
        n_s[pl.ds(pl.multiple_of(c * HEAD, HEAD), HEAD), :] = kw[:, HEAD:]
        o_s[rows, :] = aw[:, HEAD:]
        gl_s[pl.ds(pl.multiple_of(c * 8, 8), 8), :] = jnp.broadcast_to(jnp.exp(tot), (8, HEAD))

    def prep(gi, carry):
        cs = [gi * group + g for g in range(group)]
        rws = [pl.ds(pl.multiple_of(c * CHUNK, CHUNK), CHUNK) for c in cs]
        st = [dict(qc=q_s[r, :], kc=k_s[r, :], vc=v_s[r, :], gfc=gf_s[r, :], gbc=gb_s[r, :],
                   bfc=bf_s[r, :], bbc=bb_s[r, :]) for r in rws]
        for s in st:
            s["gc_f"] = _dot_split_rhs(lower, s["gfc"])
            s["gc_b"] = _dot_split_rhs(upper, s["gbc"])
            s["gc"] = jnp.where(is_f, s["gc_f"], s["gc_b"])
            kb = s["kc"].astype(BF16)
            kk2 = jnp.concatenate([kb, kb], axis=0)
            s["kk"] = lax.dot_general(kb, kk2, _NT, preferred_element_type=F32)
            s["qk"] = lax.dot_general(s["qc"].astype(BF16), kk2, _NT, preferred_element_type=F32)
        for s in st:
            s["gc_row"] = _dot_split_rhs(ones, jnp.where(diag, s["gc"], 0.0))
        for s in st:
            dec = jnp.exp(jnp.where(incl, s["gc"] - s["gc_row"], NEG_INF))
            beta = jnp.where(is_f, s["bfc"], s["bbc"])
            a = jnp.where(strict, beta * s["kk"] * dec, 0.0)
            s["x"] = -jnp.concatenate([jnp.where(is_f, a, 0.0), jnp.where(is_f, 0.0, a)], axis=0)
            s["p"] = eye + s["x"]
            s["attn"] = s["qk"] * dec
        for it in range(5):
            mm = _dot3 if it < 3 else (lambda a_, b_: _dot(a_.astype(BF16), b_.astype(BF16)))
            for s in st:
                s["x"] = mm(s["x"], s["x"])
            for s in st:
                s["p"] = s["p"] + mm(s["p"], s["x"])
        for s in st:
            kc, vc, bfc, bbc = s["kc"], s["vc"], s["bfc"], s["bbc"]
            rhs = jnp.concatenate(
                [jnp.concatenate([kc * bfc * jnp.exp(s["gc_f"]), vc * bfc], axis=1),
                 jnp.concatenate([kc * bbc * jnp.exp(s["gc_b"]), vc * bbc], axis=1)], axis=0)
            s["wu"] = _dot(s["p"].astype(BF16), rhs.astype(BF16))
        for s, c, r in zip(st, cs, rws):
            finish(c, r, s["qc"], s["kc"], s["gc_f"], s["gfc"], s["attn"][:, :CHUNK], s["wu"][:CHUNK],
                   mqf_s, nf_s, glf_s, of_s)
            finish(c, r, s["qc"], s["kc"], s["gc_b"], s["gbc"], pltpu.roll(s["attn"], CHUNK, axis=1)[:, :CHUNK],
                   s["wu"][CHUNK:], mqb_s, nb_s, glb_s, obk_s)
        return carry

    lax.fori_loop(0, n_chunks // group, prep, 0)

    def one_dir(c, s, mq_s, n_s, gl_s, o_s):
        rows = pl.ds(pl.multiple_of(c * CHUNK, CHUNK), CHUNK)
        r = _dot(mq_s[pl.ds(pl.multiple_of(c * mq_rows, CHUNK), mq_rows), :], s.astype(BF16))
        o_s[rows, :] += r[HEAD:]
        gl = gl_s[pl.ds(pl.multiple_of(c * 8, 8), 1), :]
        return s * gl + (r[:HEAD] + n_s[pl.ds(pl.multiple_of(c * HEAD, HEAD), HEAD), :])

    def scan(i, carry):
        s_f, s_b = carry
        s_f = one_dir(i, s_f, mqf_s, nf_s, glf_s, of_s)
        s_b = one_dir(n_chunks - 1 - i, s_b, mqb_s, nb_s, glb_s, obk_s)
        return s_f, s_b

    if has_s0:
        init = (s0_ref[0, 0, 0], s0_ref[0, 1, 0])
    else:
        init = (jnp.zeros((HEAD, HEAD), F32), jnp.zeros((HEAD, HEAD), F32))
    s_f, s_b = lax.fori_loop(0, n_chunks, scan, init)
    if want_state:
        st_ref[0, 0, 0] = s_f
        st_ref[0, 1, 0] = s_b

    o = of_s[...] + obk_s[...]
    ob_ref[0] = (_rms(o, nw_ref[...]) * zs_ref[0]).astype(BF16)


def _dn_call(qkv3, gates3, zs3, conv_w, norm_w, cos, sin, s0, want_state):
    b, seq, _ = qkv3.shape
    rope = cos is not None
    has_s0 = s0 is not None
    n_chunks = seq // CHUNK
    col = lambda off: pl.BlockSpec((1, seq, HEAD), lambda bi, h: (bi, 0, off + h))
    cw = lambda off: pl.BlockSpec((3, HEAD), lambda bi, h: (0, off + h))
    in_specs = [col(0), col(DN_HEADS), col(2 * DN_HEADS),
                pl.BlockSpec((1, seq, HEAD), lambda bi, h: (bi, 0, 0)),
                pl.BlockSpec((1, seq, HEAD), lambda bi, h: (bi, 0, h)),
                cw(0), cw(DN_HEADS), cw(2 * DN_HEADS),
                pl.BlockSpec((1, HEAD), lambda bi, h: (0, 0))]
    args = [qkv3, qkv3, qkv3, gates3, zs3, conv_w, conv_w, conv_w, norm_w]
    if rope:
        in_specs += [pl.BlockSpec((seq, HEAD), lambda bi, h: (0, 0))] * 2
        args += [cos, sin]
    st_spec = pl.BlockSpec((1, 2, 1, HEAD, HEAD), lambda bi, h: (bi, 0, h, 0, 0))
    if has_s0:
        in_specs.append(st_spec)
        args.append(s0)
    out_specs = [pl.BlockSpec((1, seq, HEAD), lambda bi, h: (bi, 0, h))]
    out_shape = [jax.ShapeDtypeStruct((b, seq, DN_HEADS * HEAD), BF16)]
    if want_state:
        out_specs.append(st_spec)
        out_shape.append(jax.ShapeDtypeStruct((b, 2, DN_HEADS, HEAD, HEAD), F32))
    big = pltpu.VMEM((seq, HEAD), F32)
    scratch = [big, big, big, big, big, big, big,
               pltpu.VMEM((3 * seq, HEAD), BF16), pltpu.VMEM((3 * seq, HEAD), BF16),
               pltpu.VMEM((2 * seq, HEAD), F32), pltpu.VMEM((2 * seq, HEAD), F32),
               pltpu.VMEM((8 * n_chunks, HEAD), F32), pltpu.VMEM((8 * n_chunks, HEAD), F32),
               big, big]
    res = pl.pallas_call(
        functools.partial(_dn_kernel, seq=seq, rope=rope, has_s0=has_s0, want_state=want_state),
        grid=(b, DN_HEADS),
        in_specs=in_specs,
        out_specs=out_specs,
        out_shape=out_shape,
        scratch_shapes=scratch,
        compiler_params=_params(("parallel", "parallel")),
        name="deltanet",
    )(*args)
    return res if want_state else (res[0], None)


def _out_kernel(oa_ref, ob_ref, x_ref, mod_ref, wo1_ref, wo2_ref, gpm_ref, gpf_ref, wq_ref, keys_ref,
                x1_ref, h2_ref, l1_ref, e1_ref, r2_ref, e2_ref, sc_s):
    mod = mod_ref[0]
    gate1 = mod[:, 2 * D:3 * D]
    shift2 = mod[:, 3 * D:4 * D]
    scale2 = mod[:, 4 * D:5 * D]
    y = _dot(oa_ref[...], wo1_ref[...]) + _dot(ob_ref[...], wo2_ref[...])
    x1 = x_ref[...] + gate1 * _rms(y, gpm_ref[...])
    x1_ref[...] = x1
    h2f = _rms(x1, gpf_ref[...]) * (1.0 + scale2) + shift2
    h2 = h2f.astype(BF16)
    h2_ref[...] = pltpu.bitcast(h2f.T.astype(BF16), jnp.uint32)
    q = _dot(h2, wq_ref[...]).astype(BF16)
    for hp in range(2 * PEER_HEADS):
        sc_s[hp] = lax.dot_general(keys_ref[hp], q[:, hp * HEAD:(hp + 1) * HEAD], _NT,
                                   preferred_element_type=F32)
    _select(sc_s, l1_ref, e1_ref, r2_ref, e2_ref)


def _out_call(oa, ob, x2, mod3, mod_row, wo1, wo2, gpm, gpf, wq, keys):
    t = x2.shape[0]
    tm = TM_IN
    full = lambda a: pl.BlockSpec(a.shape, lambda i: (0,) * a.ndim)
    table = pl.BlockSpec((PEER_HEADS, NKEYS, tm), lambda i: (0, 0, i))
    table_pk = pl.BlockSpec((PEER_HEADS, NKEYS // 2, tm), lambda i: (0, 0, i))
    f32 = jax.ShapeDtypeStruct((PEER_HEADS, NKEYS, t), F32)
    packed = jax.ShapeDtypeStruct((PEER_HEADS, NKEYS // 2, t), jnp.uint32)
    return pl.pallas_call(
        _out_kernel,
        grid=(t // tm,),
        in_specs=[pl.BlockSpec((tm, 512), lambda i: (i, 0)),
                  pl.BlockSpec((tm, 512), lambda i: (i, 0)),
                  pl.BlockSpec((tm, D), lambda i: (i, 0)),
                  pl.BlockSpec((1, 1, 6 * D), lambda i: (mod_row(i, tm), 0, 0)),
                  full(wo1), full(wo2), full(gpm), full(gpf), full(wq), full(keys)],
        out_specs=[pl.BlockSpec((tm, D), lambda i: (i, 0)),
                   pl.BlockSpec((D // 2, tm), lambda i: (0, i)),
                   table, table, table_pk, table_pk],
        out_shape=[jax.ShapeDtypeStruct((t, D), F32),
                   jax.ShapeDtypeStruct((D // 2, t), jnp.uint32),
                   f32, f32, packed, packed],
        scratch_shapes=[pltpu.VMEM((2 * PEER_HEADS, NKEYS, tm), F32)],
        compiler_params=_params(("parallel",)),
        name="out_proj",
    )(oa, ob, x2, mod3, wo1, wo2, gpm, gpf, wq, keys)


def _top16(x):
    vals = []
    rank = jnp.full(x.shape, float(TOPK), F32)
    for r in range(TOPK):
        m = jnp.max(x, axis=0, keepdims=True)
        vals.append(m)
        hit = x == m
        rank = jnp.where(hit, float(r), rank)
        if r + 1 < TOPK:
            x = jnp.where(hit, NEG_INF, x)
    return vals, rank


def _select(s_ref, l1_ref, e1_ref, r2_ref, e2_ref):
    tl = s_ref.shape[-1]
    row16 = lax.broadcasted_iota(jnp.int32, (TOPK, tl), 0)
    row8 = lax.broadcasted_iota(jnp.int32, (8, tl), 0)

    def head(h, carry):
        s1 = s_ref[2 * h]
        s2 = s_ref[2 * h + 1]
        a, _ = _top16(s1)
        b, rank2 = _top16(s2)
        amat = jnp.concatenate(a, axis=0)
        bmat = jnp.concatenate(b, axis=0)
        a8, b8 = amat[:8], bmat[:8]
        cands = [a[0] + bmat,
                 jnp.where(row16 >= 2, amat + b[0], NEG_INF),
                 a[1] + b8,
                 jnp.where(row8 >= 2, a8 + b[1], NEG_INF),
                 jnp.where((row8 >= 2) & (row8 <= 4), a[2] + b8, NEG_INF),
                 jnp.where((row8 >= 2) & (row8 <= 3), a[3] + b8, NEG_INF),
                 jnp.where(row8 == 2, a[4] + b8, NEG_INF)]
        work = cands
        thr = None
        for r in range(TOPK):
            m16 = jnp.max(jnp.maximum(work[0], work[1]), axis=0, keepdims=True)
            m8 = jnp.max(functools.reduce(jnp.maximum, work[2:]), axis=0, keepdims=True)
            thr = jnp.maximum(m16, m8)
            if r + 1 < TOPK:
                work = [jnp.where(w == thr, NEG_INF, w) for w in work]
        m0 = a[0] + b[0]
        z = functools.reduce(
            jnp.add, [jnp.sum(jnp.where(cd >= thr, jnp.exp(cd - m0), 0.0), axis=0, keepdims=True) for cd in cands])
        cnt = jnp.zeros_like(s1)
        for j in range(TOPK):
            cnt = cnt + jnp.where(s1 + b[j] >= thr, 1.0, 0.0)
        l1_ref[h] = cnt
        r2_ref[h] = pltpu.bitcast(rank2.astype(BF16), jnp.uint32)
        e1_ref[h] = jnp.exp(s1 - a[0]) / z
        e2_ref[h] = pltpu.bitcast(jnp.exp(s2 - b[0]).astype(BF16), jnp.uint32)
        return carry

    lax.fori_loop(0, PEER_HEADS, head, 0)


def _peer_kernel(h2_ref, u_ref, vt_ref, l1a_ref, e1a_ref, l1b_ref, e1b_ref, r2a_ref, e2a_ref, r2b_ref, e2b_ref,
                 acc_ref, at_a, at_b, ct_a, ct_b, *, n_chunks, n_items):
    s = pl.program_id(0)
    ec = u_ref.shape[0]
    tm = h2_ref.shape[1]
    pk = 16
    n_pv = NKEYS // pk
    pair = 2

    @pl.when(s == 0)
    def _():
        at_b[...] = jnp.zeros_like(at_b)
        ct_a[...] = jnp.zeros_like(ct_a)

    @pl.when((s == 0) | (lax.rem(2 * s - 2, n_chunks) == 0))
    def _():
        acc_ref[...] = jnp.zeros_like(acc_ref)

    def coefficients(c0, at_ref, ct_ref, l1_ref, e1_ref, r2_ref, e2_ref, valid):
        for lh in range(TB_PEER // HEAD):
            cols = pl.ds(pl.multiple_of(c0 + lh * HEAD, HEAD), HEAD)
            for i0 in range(0, ec // NKEYS, pair):
                bcast = lambda row: jnp.broadcast_to(row, (pk, HEAD)).astype(BF16)
                g = [[None] * n_pv for _ in range(pair)]
                for h in range(PEER_HEADS):
                    l1 = [bcast(l1_ref[h, i0 + d:i0 + d + 1, cols]) for d in range(pair)]
                    e1 = [bcast(e1_ref[h, i0 + d:i0 + d + 1, cols] * valid) for d in range(pair)]
                    for pv in range(n_pv):
                        keys2 = slice(pv * pk // 2, (pv + 1) * pk // 2)
                        r2 = pltpu.bitcast(r2_ref[h, keys2, cols], BF16)
                        e2 = pltpu.bitcast(e2_ref[h, keys2, cols], BF16)
                        for d in range(pair):
                            term = jnp.where(r2 < l1[d], e2, 0.0) * e1[d]
                            g[d][pv] = term if g[d][pv] is None else g[d][pv] + term
                for d in range(pair):
                    for pv in range(n_pv):
                        rows = slice((i0 + d) * NKEYS + pv * pk, (i0 + d) * NKEYS + (pv + 1) * pk)
                        a = at_ref[rows, cols]
                        th = jnp.tanh((a * (_GELU_C1 + _GELU_C2 * (a * a))).astype(BF16))
                        hx = 0.5 * a.astype(BF16)
                        ct_ref[rows, cols] = g[d][pv] * (hx + hx * th)

    def phase(half, at_w, at_r, ct_w, ct_r, l1_ref, e1_ref, r2_ref, e2_ref, valid):
        def body(lb, carry):
            c0 = pl.multiple_of(lb * TB_PEER, TB_PEER)
            tcols = pl.ds(c0, TB_PEER)
            h2t = pltpu.bitcast(h2_ref[:, tcols], BF16)
            u = pltpu.bitcast(u_ref[half * ec // 2:(half + 1) * ec // 2, :], BF16)
            at_w[:, tcols] = _dot(u, h2t)
            coefficients(c0, at_r, ct_w, l1_ref, e1_ref, r2_ref, e2_ref, valid)
            vt = pltpu.bitcast(vt_ref[:, half * ec:(half + 1) * ec], BF16)
            acc_ref[:, tcols] += _dot(vt, ct_r[:, tcols])
            return carry

        lax.fori_loop(0, tm // TB_PEER, body, 0)

    valid_a = jnp.where(s >= 1, 1.0, 0.0).astype(F32)
    valid_b = jnp.where(2 * s < n_items, 1.0, 0.0).astype(F32)
    phase(0, at_a, at_b, ct_b, ct_a, l1a_ref, e1a_ref, r2a_ref, e2a_ref, valid_a)
    phase(1, at_b, at_a, ct_a, ct_b, l1b_ref, e1b_ref, r2b_ref, e2b_ref, valid_b)


def _peer_call(h2_pk, u_pk, vt_pk, l1, e1, r2, e2, tm=TM_PEER, ec=EC_PEER):
    t = h2_pk.shape[1]
    n_exp = 2 * u_pk.shape[0]
    n = n_exp // ec
    n_tiles = t // tm
    n_items = n_tiles * n
    i1_blk = ec // NKEYS
    half = n // 2
    tile_sc = lambda s: jnp.minimum((2 * s) // n, n_tiles - 1)
    tile_a = lambda s: jnp.maximum(2 * s - 1, 0) // n
    tile_b = tile_sc
    tile_out = lambda s: jnp.maximum(2 * s - 2, 0) // n
    chunk_a = lambda s: lax.rem(2 * s - 1 + n, n)
    chunk_b = lambda s: lax.rem(2 * s, n)
    once = dict(pipeline_mode=pl.Buffered(1))
    row_a = pl.BlockSpec((PEER_HEADS, i1_blk, tm), lambda s: (0, chunk_a(s), tile_a(s)))
    row_b = pl.BlockSpec((PEER_HEADS, i1_blk, tm), lambda s: (0, chunk_b(s), tile_b(s)))
    tab_a = pl.BlockSpec((PEER_HEADS, NKEYS // 2, tm), lambda s: (0, 0, tile_a(s)), **once)
    tab_b = pl.BlockSpec((PEER_HEADS, NKEYS // 2, tm), lambda s: (0, 0, tile_b(s)), **once)
    return pl.pallas_call(
        functools.partial(_peer_kernel, n_chunks=n, n_items=n_items),
        grid=(n_items // 2 + 1,),
        in_specs=[pl.BlockSpec((D // 2, tm), lambda s: (0, tile_sc(s))),
                  pl.BlockSpec((ec, D), lambda s: (lax.rem(s, half), 0)),
                  pl.BlockSpec((D // 2, 2 * ec), lambda s: (0, lax.rem(s + half - 1, half))),
                  row_a, row_a, row_b, row_b, tab_a, tab_a, tab_b, tab_b],
        out_specs=pl.BlockSpec((D, tm), lambda s: (0, tile_out(s))),
        out_shape=jax.ShapeDtypeStruct((D, t), F32),
        scratch_shapes=[pltpu.VMEM((ec, tm), F32), pltpu.VMEM((ec, tm), F32),
                        pltpu.VMEM((ec, tm), BF16), pltpu.VMEM((ec, tm), BF16)],
        compiler_params=_params(("arbitrary",)),
        name="peer",
    )(h2_pk, u_pk, vt_pk, l1, e1, l1, e1, r2, e2, r2, e2)


def _fin_kernel(ot_ref, x1_ref, mod_ref, gpost_ref, y_ref):
    gate2 = mod_ref[0][:, 5 * D:6 * D]
    y_ref[...] = x1_ref[...] + gate2 * _rms(ot_ref[...].T, gpost_ref[...])


def _fin_call(ot, x1, mod3, mod_row, gpost):
    t = x1.shape[0]
    tm = TM_IN
    return pl.pallas_call(
        _fin_kernel,
        grid=(t // tm,),
        in_specs=[pl.BlockSpec((D, tm), lambda i: (0, i)),
                  pl.BlockSpec((tm, D), lambda i: (i, 0)),
                  pl.BlockSpec((1, 1, 6 * D), lambda i: (mod_row(i, tm), 0, 0)),
                  pl.BlockSpec((1, D), lambda i: (0, 0))],
        out_specs=pl.BlockSpec((tm, D), lambda i: (i, 0)),
        out_shape=jax.ShapeDtypeStruct((t, D), F32),
        compiler_params=_params(("parallel",)),
        name="peer_residual",
    )(ot, x1, mod3, gpost)


def _rope_tables(seq):
    pos = jnp.arange(seq)
    quarter = HEAD // 4
    inv = ROPE_BASE ** (-jnp.arange(quarter, dtype=F32) / quarter)
    ang_r = (pos // GRID_W).astype(F32)[:, None] * inv
    ang_c = (pos % GRID_W).astype(F32)[:, None] * inv
    cos = jnp.concatenate([jnp.cos(ang_r)] * 2 + [jnp.cos(ang_c)] * 2, axis=-1)
    sin = jnp.concatenate([-jnp.sin(ang_r), jnp.sin(ang_r), -jnp.sin(ang_c), jnp.sin(ang_c)], axis=-1)
    return cos, sin


def _layer(x, mod3, mod_row, s0, rope, want_state, w):
    b, seq, _ = x.shape
    x2 = x.reshape(b * seq, D)
    oa, qkv, zs, gates = _in_call(x2, mod3, mod_row, w["gpre"], w["wgu"], w["wgv"], w["wqkv"], w["wz"], w["wab"],
                                  w["ws"], w["bs"], w["alog"], w["dtb"])
    cos, sin = _rope_tables(seq) if rope else (None, None)
    ob, st = _dn_call(qkv.reshape(b, seq, -1), gates.reshape(b, seq, HEAD), zs.reshape(b, seq, -1),
                      w["conv"], w["dnorm"], cos, sin, s0, want_state)
    x1, h2, l1, e1, r2, e2 = _out_call(oa, ob.reshape(b * seq, -1), x2, mod3, mod_row, w["wo1"], w["wo2"],
                                       w["gpm"], w["gpf"], w["wq"], w["keys"])
    ot = _peer_call(h2, w["u"], w["vt"], l1, e1, r2, e2, tm=min(TM_PEER, b * seq))
    y = _fin_call(ot, x1, mod3, mod_row, w["gpost"])
    return y.reshape(b, seq, D), st


def kernel(x_prompt, x_sample, state_delta, c, c_ctx, w_mod, b_mod, g_pre_mix, g_post_mix, g_pre_ffn, g_post_ffn,
           w_in, w_out, gm_ws, gm_bs, dn_conv, dn_a_log, dn_dt_bias, dn_norm, peer_wq, peer_keys, peer_u, peer_v):
    depth = w_mod.shape[0]
    dec_batch, dec_seq, _ = x_sample.shape
    xp, xs = x_prompt, x_sample
    states = []
    cc = jnp.zeros((16, D), F32).at[0].set(c_ctx).at[1:1 + dec_batch].set(c)
    for l in range(depth):
        mod3 = _mod_call(cc, w_mod[l], b_mod[l][None, :]).reshape(16, 1, 6 * D)
        wi = w_in[l]
        pad16 = lambda v: jnp.zeros((1, HEAD), F32).at[0, :v.size].set(v.reshape(-1))
        w = {
            "gpre": g_pre_mix[l][None, :], "gpm": g_post_mix[l][None, :], "gpf": g_pre_ffn[l][None, :],
            "gpost": g_post_ffn[l][None, :],
            "wgu": wi[:, 0:512].astype(BF16), "wgv": wi[:, 512:1024].astype(BF16),
            "wqkv": wi[:, 1024:2560].astype(BF16), "wz": wi[:, 2560:3072].astype(BF16),
            "wab": jnp.zeros((D, HEAD), F32).at[:, :16].set(wi[:, 3072:3088]).astype(BF16),
            "ws": gm_ws[l].astype(BF16),
            "bs": jnp.broadcast_to(gm_bs[l][:, :, None], (GM_HEADS, HEAD, HEAD)),
            "alog": pad16(dn_a_log[l]), "dtb": pad16(dn_dt_bias[l]),
            "conv": dn_conv[l], "dnorm": dn_norm[l][None, :],
            "wo1": w_out[l][:512].astype(BF16), "wo2": w_out[l][512:].astype(BF16),
            "wq": peer_wq[l].astype(BF16),
            "keys": peer_keys[l].reshape(2 * PEER_HEADS, NKEYS, HEAD).astype(BF16),
            "u": _pack_call(peer_u[l]), "vt": _pack_call(peer_v[l], transpose=True),
        }
        xp, s_ctx = _layer(xp, mod3, lambda i, tm: 0, None, False, True, w)
        xs, _ = _layer(xs, mod3, lambda i, tm: 1 + (i * tm) // dec_seq, state_delta[:, l], True, False, w)
        states.append(s_ctx.astype(x_prompt.dtype))
    return xp, xs, jnp.stack(states, axis=1)
```
